```python
import math
import jax, jax.numpy as jnp
from jax import lax
import numpy as np

D_MODEL = 1024
BATCH = 8
SEQ = 4096
DEPTH = 1

N_META = 16
BLOCK = 128
ROPE_THETA = 10000.0
EPS = 1e-6

A_HEADS = 4
A_HEAD_DIM = 64
A_V_DIM = 2 * A_HEAD_DIM
A_QK_WIDTH = A_HEADS * 2 * A_HEAD_DIM
A_OUT = A_HEADS * A_V_DIM

B_HEADS = 8
B_HEAD_DIM = 64
B_OUT = B_HEADS * B_HEAD_DIM
IDX_HEADS = 8
IDX_DIM = 64
IDX_ROPE_DIM = 32
TOPK_CAP = 256

PEER_HEADS = 8
N_KEYS = 128
N_EXPERTS = N_KEYS * N_KEYS
PEER_KEY_DIM = 128
PEER_HALF = PEER_KEY_DIM // 2
PEER_TOPK = 16

SPLIT_SIZES = (
    A_QK_WIDTH, A_QK_WIDTH, A_OUT,
    B_OUT, B_HEAD_DIM, B_HEAD_DIM,
    IDX_HEADS * IDX_DIM, IDX_DIM, IDX_HEADS,
    D_MODEL, D_MODEL,
)
IN_WIDTH = int(sum(SPLIT_SIZES))
SPLIT_POINTS = tuple(int(v) for v in np.cumsum(SPLIT_SIZES)[:-1])

kernel_name = "hybrid_diffattn_dsa_peer_block"


def rms_norm(x, gain):
    xf = x.astype(jnp.float32)
    y = xf * lax.rsqrt(jnp.mean(xf * xf, axis=-1, keepdims=True) + EPS)
    return (y * gain.astype(jnp.float32)).astype(x.dtype)


def rope_tables(t_len, dim):
    inv = ROPE_THETA ** (-jnp.arange(0, dim, 2, dtype=jnp.float32) / dim)
    ang = jnp.arange(t_len, dtype=jnp.float32)[:, None] * inv[None, :]
    return jnp.cos(ang), jnp.sin(ang)


def apply_rope(x, cos, sin):
    half = x.shape[-1] // 2
    x1, x2 = x[..., :half], x[..., half:]
    c = cos[None, :, None, :].astype(x.dtype)
    s = sin[None, :, None, :].astype(x.dtype)
    return jnp.concatenate([x1 * c - x2 * s, x2 * c + x1 * s], axis=-1)


def masked_softmax(scores, mask):
    s = jnp.where(mask, scores.astype(jnp.float32), -jnp.inf)
    return jax.nn.softmax(s, axis=-1)


def diff_attention(q1, q2, k1, k2, v, lam):
    bsz, t_len, n_h, dv = v.shape
    n_blocks = t_len // BLOCK
    scale = A_HEAD_DIM ** -0.5
    key_pos = jnp.arange(t_len)

    def one_block(i):
        start = i * BLOCK
        qb1 = lax.dynamic_slice_in_dim(q1, start, BLOCK, axis=1)
        qb2 = lax.dynamic_slice_in_dim(q2, start, BLOCK, axis=1)
        causal = key_pos[None, :] <= (start + jnp.arange(BLOCK))[:, None]
        p1 = masked_softmax(jnp.einsum('bqhd,bkhd->bhqk', qb1, k1) * scale, causal)
        p2 = masked_softmax(jnp.einsum('bqhd,bkhd->bhqk', qb2, k2) * scale, causal)
        return jnp.einsum('bhqk,bkhd->bqhd', (p1 - lam * p2).astype(v.dtype), v)

    out = lax.map(one_block, jnp.arange(n_blocks))
    return out.transpose(1, 0, 2, 3, 4).reshape(bsz, t_len, n_h, dv)


def dsa_attention(q, k, v, iq, ik, iw, k_sel):
    bsz, t_len, n_h, d = q.shape
    n_blocks = t_len // BLOCK
    scale = B_HEAD_DIM ** -0.5
    key_pos = jnp.arange(t_len)

    def one_block(i):
        start = i * BLOCK
        qpos = start + jnp.arange(BLOCK)
        qb = lax.dynamic_slice_in_dim(q, start, BLOCK, axis=1)
        iqb = lax.dynamic_slice_in_dim(iq, start, BLOCK, axis=1)
        iwb = lax.dynamic_slice_in_dim(iw, start, BLOCK, axis=1)
        causal = key_pos[None, :] <= qpos[:, None]
        logits = jax.nn.relu(jnp.einsum('bqhd,bkd->bqhk', iqb, ik))
        index_score = jnp.einsum('bqh,bqhk->bqk', iwb, logits)
        index_score = jnp.where(causal[None], index_score.astype(jnp.float32), -jnp.inf)
        _, top_idx = lax.top_k(index_score, k_sel)
        valid = top_idx <= qpos[None, :, None]
        flat = top_idx.reshape(bsz, BLOCK * k_sel)[..., None]
        ks = jnp.take_along_axis(k, flat, axis=1).reshape(bsz, BLOCK, k_sel, d)
        vs = jnp.take_along_axis(v, flat, axis=1).reshape(bsz, BLOCK, k_sel, d)
        p = masked_softmax(jnp.einsum('bqhd,bqkd->bqhk', qb, ks) * scale, valid[:, :, None, :])
        return jnp.einsum('bqhk,bqkd->bqhd', p.astype(v.dtype), vs)

    out = lax.map(one_block, jnp.arange(n_blocks))
    return out.transpose(1, 0, 2, 3, 4).reshape(bsz, t_len, n_h, d)


def peer_ffn(xn, w_query, sub_keys, u, v):
    n_tok, d_model = xn.shape

    def one_block(xb):
        q = (xb @ w_query).reshape(BLOCK, PEER_HEADS, 2, PEER_HALF)
        s = jnp.einsum('thcd,hckd->thck', q, sub_keys)
        s1, i1 = lax.top_k(s[:, :, 0], PEER_TOPK)
        s2, i2 = lax.top_k(s[:, :, 1], PEER_TOPK)
        cand_s = (s1[..., :, None] + s2[..., None, :]).reshape(BLOCK, PEER_HEADS, PEER_TOPK * PEER_TOPK)
        cand_i = (i1[..., :, None] * N_KEYS + i2[..., None, :]).reshape(BLOCK, PEER_HEADS, PEER_TOPK * PEER_TOPK)
        top_s, top_c = lax.top_k(cand_s, PEER_TOPK)
        expert = jnp.take_along_axis(cand_i, top_c, axis=-1)
        g = jax.nn.softmax(top_s.astype(jnp.float32), axis=-1).astype(xb.dtype)
        ue = jnp.take(u, expert, axis=0)
        ve = jnp.take(v, expert, axis=0)
        act = jax.nn.gelu(jnp.einsum('thkd,td->thk', ue, xb), approximate=False)
        return jnp.einsum('thk,thkd->td', g * act, ve)

    out = lax.map(one_block, xn.reshape(n_tok // BLOCK, BLOCK, d_model))
    return out.reshape(n_tok, d_model)


def setup_inputs(seed: int = 0) -> dict:
    key = jax.random.key(seed)
    ks = jax.random.split(key, 24)
    f32 = jnp.float32

    def nrm(k, shape, scale):
        return jax.random.normal(k, shape, f32) * scale

    def gain(k, shape):
        return 1.0 + 0.02 * jax.random.normal(k, shape, f32)

    return {
        "x": nrm(ks[0], (BATCH, SEQ, D_MODEL), 1.0),
        "meta_tokens": nrm(ks[1], (N_META, D_MODEL), 1.0),
        "mix_norm_gain": gain(ks[2], (DEPTH, D_MODEL)),
        "w_in": nrm(ks[3], (DEPTH, D_MODEL, IN_WIDTH), D_MODEL ** -0.5),
        "a_q_norm_gain": gain(ks[4], (DEPTH, A_HEAD_DIM)),
        "a_k_norm_gain": gain(ks[5], (DEPTH, A_HEAD_DIM)),
        "a_lambda_q1": nrm(ks[6], (DEPTH, A_HEAD_DIM), 0.1),
        "a_lambda_k1": nrm(ks[7], (DEPTH, A_HEAD_DIM), 0.1),
        "a_lambda_q2": nrm(ks[8], (DEPTH, A_HEAD_DIM), 0.1),
        "a_lambda_k2": nrm(ks[9], (DEPTH, A_HEAD_DIM), 0.1),
        "a_subln_gain": gain(ks[10], (DEPTH, A_V_DIM)),
        "w_branch_a": nrm(ks[11], (DEPTH, A_OUT, D_MODEL), A_OUT ** -0.5),
        "b_q_norm_gain": gain(ks[12], (DEPTH, B_HEAD_DIM)),
        "b_k_norm_gain": gain(ks[13], (DEPTH, B_HEAD_DIM)),
        "w_branch_b": nrm(ks[14], (DEPTH, B_OUT, D_MODEL), B_OUT ** -0.5),
        "w_out": nrm(ks[15], (DEPTH, D_MODEL, D_MODEL), D_MODEL ** -0.5),
        "ffn_norm_gain": gain(ks[16], (DEPTH, D_MODEL)),
        "peer_w_query": nrm(ks[17], (DEPTH, D_MODEL, PEER_HEADS * PEER_KEY_DIM), D_MODEL ** -0.5),
        "peer_sub_keys": nrm(ks[18], (DEPTH, PEER_HEADS, 2, N_KEYS, PEER_HALF), PEER_HALF ** -0.5),
        "peer_u": nrm(ks[19], (DEPTH, N_EXPERTS, D_MODEL), D_MODEL ** -0.5),
        "peer_v": nrm(ks[20], (DEPTH, N_EXPERTS, D_MODEL), (PEER_HEADS * PEER_TOPK) ** -0.5),
    }


def reference(x, meta_tokens, mix_norm_gain, w_in, a_q_norm_gain, a_k_norm_gain,
              a_lambda_q1, a_lambda_k1, a_lambda_q2, a_lambda_k2, a_subln_gain, w_branch_a,
              b_q_norm_gain, b_k_norm_gain, w_branch_b, w_out, ffn_norm_gain,
              peer_w_query, peer_sub_keys, peer_u, peer_v):
    bsz, seq, d_model = x.shape
    t_real = N_META + seq
    t_pad = ((t_real + BLOCK - 1) // BLOCK) * BLOCK
    k_sel = min(TOPK_CAP, seq // 4)

    meta = jnp.broadcast_to(meta_tokens.astype(x.dtype)[None], (bsz, N_META, d_model))
    pad = jnp.zeros((bsz, t_pad - t_real, d_model), x.dtype)
    h = jnp.concatenate([meta, x, pad], axis=1)

    cos, sin = rope_tables(t_pad, A_HEAD_DIM)
    icos, isin = rope_tables(t_pad, IDX_ROPE_DIM)

    for layer in range(DEPTH):
        lambda_init = 0.8 - 0.6 * math.exp(-0.3 * layer)
        xn = rms_norm(h, mix_norm_gain[layer])
        proj = xn @ w_in[layer]
        aq, ak, av, bq, bk, bv, iq, ik, iw, gate_a, gate_b = jnp.split(proj, SPLIT_POINTS, axis=-1)

        aq = apply_rope(rms_norm(aq.reshape(bsz, t_pad, 2 * A_HEADS, A_HEAD_DIM), a_q_norm_gain[layer]), cos, sin)
        ak = apply_rope(rms_norm(ak.reshape(bsz, t_pad, 2 * A_HEADS, A_HEAD_DIM), a_k_norm_gain[layer]), cos, sin)
        aq = aq.reshape(bsz, t_pad, A_HEADS, 2, A_HEAD_DIM)
        ak = ak.reshape(bsz, t_pad, A_HEADS, 2, A_HEAD_DIM)
        av = av.reshape(bsz, t_pad, A_HEADS, A_V_DIM)
        lam = (jnp.exp(jnp.sum(a_lambda_q1[layer].astype(jnp.float32) * a_lambda_k1[layer].astype(jnp.float32)))
               - jnp.exp(jnp.sum(a_lambda_q2[layer].astype(jnp.float32) * a_lambda_k2[layer].astype(jnp.float32)))
               + lambda_init)
        a_out = diff_attention(aq[..., 0, :], aq[..., 1, :], ak[..., 0, :], ak[..., 1, :], av, lam)
        a_out = rms_norm(a_out, a_subln_gain[layer]) * (1.0 - lambda_init)
        y_a = a_out.reshape(bsz, t_pad, A_OUT) @ w_branch_a[layer]

        bq = apply_rope(rms_norm(bq.reshape(bsz, t_pad, B_HEADS, B_HEAD_DIM), b_q_norm_gain[layer]), cos, sin)
        bk = apply_rope(rms_norm(bk.reshape(bsz, t_pad, 1, B_HEAD_DIM), b_k_norm_gain[layer]), cos, sin)[:, :, 0]
        iq = iq.reshape(bsz, t_pad, IDX_HEADS, IDX_DIM)
        iq = jnp.concatenate([apply_rope(iq[..., :IDX_ROPE_DIM], icos, isin), iq[..., IDX_ROPE_DIM:]], axis=-1)
        ik = ik[:, :, None, :]
        ik = jnp.concatenate([apply_rope(ik[..., :IDX_ROPE_DIM], icos, isin), ik[..., IDX_ROPE_DIM:]], axis=-1)[:, :, 0]
        iw = iw * (IDX_HEADS ** -0.5 * IDX_DIM ** -0.5)
        b_out = dsa_attention(bq, bk, bv, iq, ik, iw, k_sel)
        y_b = b_out.reshape(bsz, t_pad, B_OUT) @ w_branch_b[layer]

        mixed = jax.nn.sigmoid(gate_a) * y_a + jax.nn.sigmoid(gate_b) * y_b
        h = h + mixed @ w_out[layer]

        hn = rms_norm(h, ffn_norm_gain[layer]).reshape(bsz * t_pad, d_model)
        h = h + peer_ffn(hn, peer_w_query[layer], peer_sub_keys[layer], peer_u[layer], peer_v[layer]).reshape(bsz, t_pad, d_model)

    return h[:, N_META:N_META + seq]
```

```python
import functools
import math

import jax
import jax.numpy as jnp
from jax import lax
from jax.experimental import pallas as pl
from jax.experimental.pallas import tpu as pltpu

F32 = jnp.float32
BF16 = jnp.bfloat16
I32 = jnp.int32

N_META_TOKENS = 16
SEQ_BLOCK = 128
ROPE_BASE = 10000.0
NORM_EPS = 1e-6
HEAD_DIM = 64
LANES = 128
A_HEADS = 4
B_HEADS = 8
IDX_HEADS = 8
IDX_ROPE = 32
TOPK_LIMIT = 256
PEER_HEADS = 8
PEER_TOPK = 16
MASKED_SCORE = -1e30
RUNNING_MAX_FLOOR = -1e29
INT_MIN = -(2 ** 31)
VMEM_LIMIT = 56 * 1024 * 1024


def _pick_tile(n, candidates):
    for c in candidates:
        if n % c == 0:
            return c
    raise ValueError(f"no tile in {candidates} divides {n}")


def _params(semantics):
    return pltpu.CompilerParams(dimension_semantics=semantics, vmem_limit_bytes=VMEM_LIMIT)


def _dot(a, b):
    return jnp.dot(a, b, preferred_element_type=F32)


def _dot_nt(a, b):
    return lax.dot_general(a, b, (((1,), (1,)), ((), ())), preferred_element_type=F32)


def _lane_iota(shape):
    return lax.broadcasted_iota(I32, shape, 1)


_SEC = {}
_off = 0
for _name, _w in (("aq", 512), ("ak", 512), ("av", 512), ("bq", 512), ("bk", 128), ("bv", 128),
                  ("iq", 512), ("ik", 128), ("iw", 128), ("ga", 1024), ("gb", 1024)):
    _SEC[_name] = (_off, _off + _w)
    _off += _w
IN_COLS = _off


def _head_norm(x, gain, ones_blk):
    w = x.shape[1]
    x2 = x * x
    hi = x2.astype(BF16)
    lo = (x2 - hi.astype(F32)).astype(BF16)
    m = ones_blk[:w, :w]
    ms = _dot(hi, m) + _dot(lo, m)
    return x * lax.rsqrt(ms + NORM_EPS) * gain


def _rotary(x, cos, sin_signed, half):
    w = x.shape[1]
    lane = _lane_iota(x.shape)
    fwd = pltpu.roll(x, w - half, 1)
    bwd = pltpu.roll(x, half, 1)
    partner = jnp.where((lane & half) == 0, fwd, bwd)
    return x * cos + partner * sin_signed


def _inproj_kernel(h_ref, gain_ref, w_ref, ones_ref, cos_ref, sin_ref, icos_ref, isin_ref,
                   aqg_ref, akg_ref, bqg_ref, bkg_ref,
                   aq_o, ak_o, av_o, bq_o, bk_o, bv_o, iq_o, ik_o, iw_o, ga_o, gb_o, *, iw_scale):
    h = h_ref[...]
    xn = (h * lax.rsqrt(jnp.mean(h * h, axis=-1, keepdims=True) + NORM_EPS) * gain_ref[...]).astype(BF16)

    def proj(name):
        a, b = _SEC[name]
        return _dot(xn, w_ref[:, a:b])

    ones_blk = ones_ref[...]
    cos = cos_ref[...]
    sin = sin_ref[...]
    aq_o[...] = _rotary(_head_norm(proj("aq"), aqg_ref[...], ones_blk), cos, sin, 32).astype(BF16)
    ak_o[...] = _rotary(_head_norm(proj("ak"), akg_ref[...], ones_blk), cos, sin, 32).astype(BF16)
    av_o[...] = proj("av").astype(BF16)
    bq_o[...] = _rotary(_head_norm(proj("bq"), bqg_ref[...], ones_blk), cos, sin, 32).astype(BF16)
    bk_o[...] = _rotary(_head_norm(proj("bk"), bkg_ref[...], ones_blk), cos[:, :LANES], sin[:, :LANES], 32).astype(BF16)
    bv_o[...] = proj("bv").astype(BF16)
    icos = icos_ref[...]
    isin = isin_ref[...]
    iq_o[...] = _rotary(proj("iq"), icos, isin, 16).astype(BF16)
    ik_o[...] = _rotary(proj("ik"), icos[:, :LANES], isin[:, :LANES], 16).astype(BF16)
    iw_o[...] = proj("iw") * iw_scale
    ga_o[...] = jax.nn.sigmoid(proj("ga")).astype(BF16)
    gb_o[...] = jax.nn.sigmoid(proj("gb")).astype(BF16)


def _rope_tables(t_len, dim, width_dim):
    inv = ROPE_BASE ** (-jnp.arange(0, dim, 2, dtype=F32) / dim)
    ang = jnp.arange(t_len, dtype=F32)[:, None] * inv[None, :]
    cos, sin = jnp.cos(ang), jnp.sin(ang)
    rest = width_dim - dim
    cos_h = jnp.concatenate([cos, cos, jnp.ones((t_len, rest), F32)], axis=1)
    sin_h = jnp.concatenate([-sin, sin, jnp.zeros((t_len, rest), F32)], axis=1)
    return cos_h, sin_h


def _input_projection(h2d, t_pad, mix_gain, w_in, aqg, akg, bqg, bkg):
    n_tok, d_model = h2d.shape
    tm = _pick_tile(t_pad, (384, 256, 128))
    per_seq = t_pad // tm

    sizes = (512, 512, 512, 512, 64, 64, 512, 64, IDX_HEADS, d_model, d_model)
    offs = [0]
    for s in sizes:
        offs.append(offs[-1] + s)
    cols = [w_in[:, offs[i]:offs[i + 1]] for i in range(len(sizes))]
    waq, wak, wav, wbq, wbk, wbv, wiq, wik, wiw, wga, wgb = cols
    wiw = jnp.pad(wiw, ((0, 0), (0, LANES - IDX_HEADS)))
    w_all = jnp.concatenate([waq, wak, wav, wbq, wbk, wbk, wbv, wbv, wiq, wik, wik, wiw, wga, wgb],
                            axis=1).astype(BF16)
    assert w_all.shape[1] == IN_COLS

    cos_h, sin_h = _rope_tables(t_pad, HEAD_DIM, HEAD_DIM)
    cos = jnp.tile(cos_h, (1, 8))
    sin = jnp.tile(sin_h, (1, 8))
    icos_h, isin_h = _rope_tables(t_pad, IDX_ROPE, HEAD_DIM)
    icos = jnp.tile(icos_h, (1, 8))
    isin = jnp.tile(isin_h, (1, 8))
    head_id = jnp.arange(512) // HEAD_DIM
    ones_blk = jnp.where(head_id[:, None] == head_id[None, :], 1.0 / HEAD_DIM, 0.0).astype(BF16)

    row = lambda i: (i, 0)
    fixed = lambda i: (0, 0)
    tab = lambda i: (i % per_seq, 0)
    widths = dict(aq=512, ak=512, av=512, bq=512, bk=128, bv=128, iq=512, ik=128, iw=128, ga=d_model, gb=d_model)
    out_dtypes = dict(iw=F32)
    names = ("aq", "ak", "av", "bq", "bk", "bv", "iq", "ik", "iw", "ga", "gb")
    out_shape = tuple(jax.ShapeDtypeStruct((n_tok, widths[k]), out_dtypes.get(k, BF16)) for k in names)
    out_specs = tuple(pl.BlockSpec((tm, widths[k]), row) for k in names)
    in_specs = [
        pl.BlockSpec((tm, d_model), row),
        pl.BlockSpec((1, d_model), fixed),
        pl.BlockSpec((d_model, IN_COLS), fixed),
        pl.BlockSpec((512, 512), fixed),
        pl.BlockSpec((tm, 512), tab), pl.BlockSpec((tm, 512), tab),
        pl.BlockSpec((tm, 512), tab), pl.BlockSpec((tm, 512), tab),
        pl.BlockSpec((1, 512), fixed), pl.BlockSpec((1, 512), fixed),
        pl.BlockSpec((1, 512), fixed), pl.BlockSpec((1, LANES), fixed),
    ]
    iw_scale = IDX_HEADS ** -0.5 * HEAD_DIM ** -0.5
    outs = pl.pallas_call(
        functools.partial(_inproj_kernel, iw_scale=iw_scale),
        grid=(n_tok // tm,),
        in_specs=in_specs, out_specs=out_specs, out_shape=out_shape,
        compiler_params=_params(("parallel",)),
        name="inproj",
    )(h2d, mix_gain.reshape(1, d_model), w_all, ones_blk, cos, sin, icos, isin, aqg, akg, bqg, bkg)
    return dict(zip(names, outs))


def _stack_heads(x):
    rows, width = x.shape
    lane = _lane_iota((rows, LANES))
    zero = jnp.zeros((rows, LANES), x.dtype)
    groups = []
    for c in range(width // LANES):
        blk = x[:, c * LANES:(c + 1) * LANES]
        groups.append(jnp.where(lane < HEAD_DIM, blk, zero))
        groups.append(jnp.where(lane >= HEAD_DIM, blk, zero))
    return jnp.concatenate(groups, axis=0)


def _softmax_step(s, valid, v, m, l, acc):
    s = jnp.where(valid, s, MASKED_SCORE)
    m_new = jnp.maximum(m, jnp.max(s, axis=1, keepdims=True))
    alpha = jnp.exp(m - m_new)
    p = jnp.exp(s - m_new)
    l_new = alpha * l + jnp.sum(p, axis=1, keepdims=True)
    acc_new = alpha * acc + _dot(p.astype(BF16), v)
    return m_new, l_new, acc_new


def _diff_kernel(q_ref, k_ref, v_ref, lq1_ref, lk1_ref, lq2_ref, lk2_ref, gain_ref, o_ref, *,
                 tq, tk, lambda_init):
    i = pl.program_id(2)
    qs = _stack_heads(q_ref[...])
    rows = 2 * tq
    qpos = i * tq + lax.broadcasted_iota(I32, (rows, tk), 0) % tq
    kcol = _lane_iota((rows, tk))

    def body(j, carry):
        m, l, acc = carry
        start = pl.multiple_of(j * tk, tk)
        k = k_ref[pl.ds(start, tk), :]
        v = v_ref[pl.ds(start, tk), :]
        s = _dot_nt(qs, k)
        valid = (kcol + j * tk) <= qpos
        return _softmax_step(s, valid, v, m, l, acc)

    n_chunks = ((i + 1) * tq + tk - 1) // tk
    init = (jnp.full((rows, 1), RUNNING_MAX_FLOOR, F32), jnp.zeros((rows, 1), F32), jnp.zeros((rows, LANES), F32))
    m, l, acc = lax.fori_loop(0, n_chunks, body, init)
    o = acc / l
    lam = (jnp.exp(jnp.sum(lq1_ref[...] * lk1_ref[...], axis=-1, keepdims=True))
           - jnp.exp(jnp.sum(lq2_ref[...] * lk2_ref[...], axis=-1, keepdims=True)) + lambda_init)
    d = o[:tq] - lam * o[tq:]
    d = d * lax.rsqrt(jnp.mean(d * d, axis=-1, keepdims=True) + NORM_EPS) * gain_ref[...]
    o_ref[...] = (d * (1.0 - lambda_init)).astype(BF16)


def _diff_attention(aq, ak, av, bsz, t_pad, lq1, lk1, lq2, lk2, subln_gain, lambda_init):
    tq = _pick_tile(t_pad, (384, 256, 128))
    tk = tq
    nq = t_pad // tq
    qmap = lambda b, h, i: (b * nq + i, h)
    kvmap = lambda b, h, i: (b, h)
    fixed = lambda b, h, i: (0, 0)
    vec = pl.BlockSpec((1, HEAD_DIM), fixed)
    return pl.pallas_call(
        functools.partial(_diff_kernel, tq=tq, tk=tk, lambda_init=lambda_init),
        grid=(bsz, A_HEADS, nq),
        in_specs=[pl.BlockSpec((tq, LANES), qmap), pl.BlockSpec((t_pad, LANES), kvmap),
                  pl.BlockSpec((t_pad, LANES), kvmap), vec, vec, vec, vec, pl.BlockSpec((1, LANES), fixed)],
        out_specs=pl.BlockSpec((tq, LANES), qmap),
        out_shape=jax.ShapeDtypeStruct(aq.shape, BF16),
        compiler_params=_params(("parallel", "parallel", "arbitrary")),
        name="diff_attn",
    )(aq, ak, av, lq1, lk1, lq2, lk2, subln_gain)


def _dsa_kernel(iq_ref, iw_ref, ik_ref, bq_ref, bk_ref, bv_ref, o_ref, key_ref, *, tq, tk, k_sel, idx_bits):
    i = pl.program_id(1)
    n_chunks = ((i + 1) * tq + tk - 1) // tk
    n_lane_blocks = n_chunks * (tk // LANES)
    qpos_col = i * tq + lax.broadcasted_iota(I32, (tq, 1), 0)
    kcol = _lane_iota((tq, tk))

    iqs = _stack_heads(iq_ref[...])
    iw = iw_ref[...]
    iw_col = jnp.concatenate([iw[:, h:h + 1] for h in range(IDX_HEADS)], axis=0)

    def score_body(j, _):
        start = pl.multiple_of(j * tk, tk)
        logits = jnp.maximum(_dot_nt(iqs, ik_ref[pl.ds(start, tk), :]), 0.0) * iw_col
        score = logits[0:tq]
        for h in range(1, IDX_HEADS):
            score = score + logits[h * tq:(h + 1) * tq]
        score = jnp.where((kcol + j * tk) <= qpos_col, score, -jnp.inf)
        bits = lax.bitcast_convert_type(score, I32)
        key_ref[:, pl.ds(start, tk)] = jnp.where(bits < 0, (bits ^ 0x7FFFFFFF) + 1, bits)
        return 0

    lax.fori_loop(0, n_chunks, score_body, 0)

    def count(pred_fn):
        def body(c, acc):
            start = pl.multiple_of(c * LANES, LANES)
            blk = key_ref[:, pl.ds(start, LANES)]
            return acc + jnp.where(pred_fn(blk, c * LANES), 1.0, 0.0)
        acc = lax.fori_loop(0, n_lane_blocks, body, jnp.zeros((tq, LANES), F32))
        return jnp.sum(acc, axis=1, keepdims=True)

    def count_ge(cand_col):
        cand = jnp.broadcast_to(cand_col, (tq, LANES))
        return count(lambda blk, base: blk >= cand)

    kf = float(k_sel)
    thr = jnp.where(count_ge(jnp.zeros((tq, 1), I32)) >= kf, 0, INT_MIN).astype(I32)

    def bit_body(t, thr):
        cand = thr + jnp.left_shift(jnp.int32(1), 30 - t)
        return jnp.where(count_ge(cand) >= kf, cand, thr)

    thr = lax.fori_loop(0, 31, bit_body, thr)

    n_above = count_ge(thr + 1)
    n_ties_kept = kf - n_above
    thr_b = jnp.broadcast_to(thr, (tq, LANES))
    lane128 = _lane_iota((tq, LANES))

    def tie_search():
        def idx_body(t, pos):
            cand = pos + jnp.left_shift(jnp.int32(1), idx_bits - 1 - t)
            cand_b = jnp.broadcast_to(cand, (tq, LANES))
            below = count(lambda blk, base: (blk == thr_b) & ((lane128 + base) < cand_b))
            return jnp.where(below < n_ties_kept, cand, pos)
        return lax.fori_loop(0, idx_bits, idx_body, jnp.zeros((tq, 1), I32))

    has_excess_ties = jnp.max(count_ge(thr) - kf) > 0.0
    last_tie = lax.cond(has_excess_ties, tie_search, lambda: jnp.full((tq, 1), 2 ** idx_bits, I32))

    qs = _stack_heads(bq_ref[...])
    rows = B_HEADS * tq

    def attn_body(j, carry):
        m, l, acc = carry
        start = pl.multiple_of(j * tk, tk)
        key = key_ref[:, pl.ds(start, tk)]
        kidx = kcol + j * tk
        sel = ((key > thr) | ((key == thr) & (kidx <= last_tie))) & (kidx <= qpos_col)
        sel = jnp.concatenate([sel] * B_HEADS, axis=0)
        s = _dot_nt(qs, bk_ref[pl.ds(start, tk), :])
        return _softmax_step(s, sel, bv_ref[pl.ds(start, tk), :], m, l, acc)

    init = (jnp.full((rows, 1), RUNNING_MAX_FLOOR, F32), jnp.zeros((rows, 1), F32), jnp.zeros((rows, LANES), F32))
    m, l, acc = lax.fori_loop(0, n_chunks, attn_body, init)
    o = acc / l
    lane = _lane_iota((tq, LANES))
    for c in range(B_HEADS // 2):
        o_ref[:, c * LANES:(c + 1) * LANES] = jnp.where(
            lane < HEAD_DIM, o[(2 * c) * tq:(2 * c + 1) * tq], o[(2 * c + 1) * tq:(2 * c + 2) * tq]).astype(BF16)


def _dsa_attention(iq, iw, ik, bq, bk, bv, bsz, t_pad, k_sel):
    tq = SEQ_BLOCK
    tk = _pick_tile(t_pad, (384, 256, 128))
    nq = t_pad // tq
    idx_bits = max(1, (t_pad - 1).bit_length())
    qmap = lambda b, i: (b * nq + i, 0)
    kvmap = lambda b, i: (b, 0)
    return pl.pallas_call(
        functools.partial(_dsa_kernel, tq=tq, tk=tk, k_sel=k_sel, idx_bits=idx_bits),
        grid=(bsz, nq),
        in_specs=[pl.BlockSpec((tq, 512), qmap), pl.BlockSpec((tq, LANES), qmap), pl.BlockSpec((t_pad, LANES), kvmap),
                  pl.BlockSpec((tq, 512), qmap), pl.BlockSpec((t_pad, LANES), kvmap), pl.BlockSpec((t_pad, LANES), kvmap)],
        out_specs=pl.BlockSpec((tq, 512), qmap),
        out_shape=jax.ShapeDtypeStruct(bq.shape, BF16),
        scratch_shapes=[pltpu.VMEM((tq, t_pad), I32)],
        compiler_params=_params(("parallel", "arbitrary")),
        name="dsa_attn",
    )(iq, iw, ik, bq, bk, bv)


def _merge_kernel(h_ref, a_ref, b_ref, ga_ref, gb_ref, wa_ref, wb_ref, wo_ref, gain_ref, wq_ref,
                  h1_o, hn_o, pq_o):
    ya = _dot(a_ref[...], wa_ref[...])
    yb = _dot(b_ref[...], wb_ref[...])
    mixed = ga_ref[...].astype(F32) * ya + gb_ref[...].astype(F32) * yb
    h1 = h_ref[...] + _dot(mixed.astype(BF16), wo_ref[...])
    h1_o[...] = h1
    hn = (h1 * lax.rsqrt(jnp.mean(h1 * h1, axis=-1, keepdims=True) + NORM_EPS) * gain_ref[...]).astype(BF16)
    hn_o[...] = hn
    pq_o[...] = _dot(hn, wq_ref[...]).astype(BF16)


def _merge(h2d, a_out, b_out, ga, gb, w_a, w_b, w_o, ffn_gain, w_query):
    n_tok, d_model = h2d.shape
    tm = _pick_tile(n_tok, (256, 128))
    row = lambda i: (i, 0)
    fixed = lambda i: (0, 0)
    pw = w_query.shape[1]
    return pl.pallas_call(
        _merge_kernel,
        grid=(n_tok // tm,),
        in_specs=[pl.BlockSpec((tm, d_model), row), pl.BlockSpec((tm, 512), row), pl.BlockSpec((tm, 512), row),
                  pl.BlockSpec((tm, d_model), row), pl.BlockSpec((tm, d_model), row),
                  pl.BlockSpec((512, d_model), fixed), pl.BlockSpec((512, d_model), fixed),
                  pl.BlockSpec((d_model, d_model), fixed), pl.BlockSpec((1, d_model), fixed),
                  pl.BlockSpec((d_model, pw), fixed)],
        out_specs=(pl.BlockSpec((tm, d_model), row), pl.BlockSpec((tm, d_model), row), pl.BlockSpec((tm, pw), row)),
        out_shape=(jax.ShapeDtypeStruct((n_tok, d_model), F32), jax.ShapeDtypeStruct((n_tok, d_model), BF16),
                   jax.ShapeDtypeStruct((n_tok, pw), BF16)),
        compiler_params=_params(("parallel",)),
        name="merge",
    )(h2d, a_out, b_out, ga, gb, w_a.astype(BF16), w_b.astype(BF16), w_o.astype(BF16),
      ffn_gain.reshape(1, d_model), w_query.astype(BF16))


def _top_values(x, count):
    vals = []
    cur = x
    for _ in range(count):
        m = jnp.max(cur, axis=0, keepdims=True)
        vals.append(m)
        cur = jnp.where(cur == m, -jnp.inf, cur)
    return jnp.concatenate(vals, axis=0)


def _peer_routing(pq_ref, sk_ref, s1_ref, e1_ref, s2_ref, e2_ref, thr_ref, n_keys):
    def head_body(h, _):
        lo = pl.multiple_of(h * LANES, LANES)
        st = _dot_nt(sk_ref[h], pq_ref[:, pl.ds(lo, LANES)])
        s1 = st[:n_keys]
        s2 = st[n_keys:]
        v1 = _top_values(s1, PEER_TOPK)
        v2 = _top_values(s2, PEER_TOPK)
        cand = jnp.concatenate([v1[a:a + 1] + v2 for a in range(PEER_TOPK)], axis=0)
        thr = _top_values(cand, PEER_TOPK)[PEER_TOPK - 1:PEER_TOPK]
        top = v1[0:1] + v2[0:1]
        z = jnp.sum(jnp.where(cand >= thr, jnp.exp(cand - top), 0.0), axis=0, keepdims=True)
        s1_ref[h] = s1
        s2_ref[h] = s2
        e1_ref[h] = jnp.exp(s1 - v1[0:1])
        e2_ref[h] = jnp.exp(s2 - v2[0:1]) / z
        thr_ref[h] = jnp.broadcast_to(thr, (8, thr.shape[1]))
        return 0

    lax.fori_loop(0, PEER_HEADS, head_body, 0)


def _gelu_exact(x):
    return 0.5 * x * (1.0 + lax.erf(x * (1.0 / math.sqrt(2.0))))


def _peer_kernel(h1_ref, hn_ref, pq_ref, sk_ref, u_ref, vt_ref, o_ref,
                 s1_ref, e1_ref, s2_ref, e2_ref, thr_ref, act_ref, p_ref, acc_ref, *, te, n_keys):
    j = pl.program_id(1)
    tm = hn_ref.shape[0]
    rows_per_step = te // n_keys

    @pl.when(j == 0)
    def _():
        _peer_routing(pq_ref, sk_ref, s1_ref, e1_ref, s2_ref, e2_ref, thr_ref, n_keys)
        acc_ref[...] = jnp.zeros_like(acc_ref)

    act_ref[...] = _dot_nt(u_ref[...], hn_ref[...])

    def group_body(t, _):
        g = t // (tm // LANES)
        cols = pl.ds(pl.multiple_of((t % (tm // LANES)) * LANES, LANES), LANES)
        row8 = pl.ds(pl.multiple_of(j * rows_per_step + g * 8, 8), 8)
        s1_rows = [s1_ref[h, row8, cols] for h in range(PEER_HEADS)]
        e1_rows = [e1_ref[h, row8, cols] for h in range(PEER_HEADS)]
        thr = [thr_ref[h, 0:1, cols] for h in range(PEER_HEADS)]
        for k in range(8):
            w = jnp.zeros((n_keys, LANES), F32)
            for h in range(PEER_HEADS):
                pair = s2_ref[h, :, cols] + s1_rows[h][k:k + 1]
                w = w + jnp.where(pair >= thr[h], e2_ref[h, :, cols] * e1_rows[h][k:k + 1], 0.0)
            rows = pl.ds(pl.multiple_of((g * 8 + k) * n_keys, n_keys), n_keys)
            p_ref[rows, cols] = (w * _gelu_exact(act_ref[rows, cols])).astype(BF16)
        return 0

    lax.fori_loop(0, (rows_per_step // 8) * (tm // LANES), group_body, 0)
    acc_ref[...] += _dot(vt_ref[...], p_ref[...])

    @pl.when(j == pl.num_programs(1) - 1)
    def _():
        o_ref[...] = h1_ref[...] + acc_ref[...].T


def _peer(h1, hn, pq, sub_keys, peer_u, peer_v):
    n_tok, d_model = h1.shape
    n_heads, _, n_keys, half = sub_keys.shape
    n_experts = peer_u.shape[0]
    assert n_heads == PEER_HEADS and n_keys == LANES and 2 * half == LANES and n_experts == n_keys * n_keys
    tm = _pick_tile(n_tok, (512, 384, 256, 128))
    te = 2048
    zeros = jnp.zeros((n_heads, n_keys, half), sub_keys.dtype)
    sk = jnp.concatenate([jnp.concatenate([sub_keys[:, 0], zeros], axis=2),
                          jnp.concatenate([zeros, sub_keys[:, 1]], axis=2)], axis=1).astype(BF16)
    u = peer_u.astype(BF16)
    vt = peer_v.astype(BF16).T
    row = lambda i, j: (i, 0)
    route = pltpu.VMEM((PEER_HEADS, n_keys, tm), F32)
    return pl.pallas_call(
        functools.partial(_peer_kernel, te=te, n_keys=n_keys),
        grid=(n_tok // tm, n_experts // te),
        in_specs=[pl.BlockSpec((tm, d_model), row), pl.BlockSpec((tm, d_model), row),
                  pl.BlockSpec((tm, PEER_HEADS * LANES), row),
                  pl.BlockSpec((n_heads, 2 * n_keys, LANES), lambda i, j: (0, 0, 0)),
                  pl.BlockSpec((te, d_model), lambda i, j: (j, 0)),
                  pl.BlockSpec((d_model, te), lambda i, j: (0, j))],
        out_specs=pl.BlockSpec((tm, d_model), row),
        out_shape=jax.ShapeDtypeStruct((n_tok, d_model), F32),
        scratch_shapes=[route, route, route, route, pltpu.VMEM((PEER_HEADS, 8, tm), F32),
                        pltpu.VMEM((te, tm), F32), pltpu.VMEM((te, tm), BF16), pltpu.VMEM((d_model, tm), F32)],
        compiler_params=_params(("parallel", "arbitrary")),
        name="peer",
    )(h1, hn, pq, sk, u, vt)


def kernel(x, meta_tokens, mix_norm_gain, w_in, a_q_norm_gain, a_k_norm_gain, a_lambda_q1, a_lambda_k1, a_lambda_q2, a_lambda_k2, a_subln_gain, w_branch_a, b_q_norm_gain, b_k_norm_gain, w_branch_b, w_out, ffn_norm_gain, peer_w_query, peer_sub_keys, peer_u, peer_v):
    bsz, seq, d_model = x.shape
    depth = w_in.shape[0]
    t_real = N_META_TOKENS + seq
    t_pad = ((t_real + SEQ_BLOCK - 1) // SEQ_BLOCK) * SEQ_BLOCK
    k_sel = min(TOPK_LIMIT, seq // 4)

    meta = jnp.broadcast_to(meta_tokens.astype(x.dtype)[None], (bsz, N_META_TOKENS, d_model))
    pad = jnp.zeros((bsz, t_pad - t_real, d_model), x.dtype)
    h = jnp.concatenate([meta, x, pad], axis=1).reshape(bsz * t_pad, d_model)

    q_scale = HEAD_DIM ** -0.5
    for layer in range(depth):
        lambda_init = 0.8 - 0.6 * math.exp(-0.3 * layer)
        aqg = (jnp.tile(a_q_norm_gain[layer], 8) * q_scale).reshape(1, 512)
        akg = jnp.tile(a_k_norm_gain[layer], 8).reshape(1, 512)
        bqg = (jnp.tile(b_q_norm_gain[layer], 8) * q_scale).reshape(1, 512)
        bkg = jnp.tile(b_k_norm_gain[layer], 2).reshape(1, LANES)
        p = _input_projection(h, t_pad, mix_norm_gain[layer], w_in[layer], aqg, akg, bqg, bkg)
        vec = lambda a: a[layer].astype(F32).reshape(1, HEAD_DIM)
        a_out = _diff_attention(p["aq"], p["ak"], p["av"], bsz, t_pad, vec(a_lambda_q1), vec(a_lambda_k1),
                                vec(a_lambda_q2), vec(a_lambda_k2), a_subln_gain[layer].reshape(1, LANES), lambda_init)
        b_out = _dsa_attention(p["iq"], p["iw"], p["ik"], p["bq"], p["bk"], p["bv"], bsz, t_pad, k_sel)
        h1, hn, pq = _merge(h, a_out, b_out, p["ga"], p["gb"], w_branch_a[layer], w_branch_b[layer], w_out[layer],
                            ffn_norm_gain[layer], peer_w_query[layer])
        h = _peer(h1, hn, pq, peer_sub_keys[layer], peer_u[layer], peer_v[layer])

    return h.reshape(bsz, t_pad, d_model)[:, N_META_TOKENS:N_META_TOKENS + seq]
```

```python
import functools
import math

import jax
import jax.numpy as jnp
from jax import lax
from jax.experimental import pallas as pl
from jax.experimental.pallas import tpu as pltpu

F32 = jnp.float32
BF16 = jnp.bfloat16
I32 = jnp.int32

N_META_TOKENS = 16
SEQ_BLOCK = 128
ROPE_BASE = 10000.0
NORM_EPS = 1e-6
HEAD_DIM = 64
LANES = 128
A_HEADS = 4
B_HEADS = 8
IDX_HEADS = 8
IDX_ROPE = 32
TOPK_LIMIT = 256
PEER_HEADS = 8
PEER_TOPK = 16
MASKED_SCORE = -1e30
RUNNING_MAX_FLOOR = -1e29
INT_MIN = -(2 ** 31)
LOG2_E = math.log2(math.e)
VMEM_LIMIT = 56 * 1024 * 1024


def _pick_tile(n, candidates):
    for c in candidates:
        if n % c == 0:
            return c
    raise ValueError(f"no tile in {candidates} divides {n}")


def _params(semantics):
    return pltpu.CompilerParams(dimension_semantics=semantics, vmem_limit_bytes=VMEM_LIMIT)


def _dot(a, b):
    return jnp.dot(a, b, preferred_element_type=F32)


def _dot_nt(a, b):
    return lax.dot_general(a, b, (((1,), (1,)), ((), ())), preferred_element_type=F32)


def _lane_iota(shape):
    return lax.broadcasted_iota(I32, shape, 1)


_SEC = {}
_off = 0
for _name, _w in (("aq", 512), ("ak", 512), ("av", 512), ("bq", 512), ("bk", 128), ("bv", 128),
                  ("iq", 512), ("ik", 128), ("iw", 128), ("ga", 1024), ("gb", 1024)):
    _SEC[_name] = (_off, _off + _w)
    _off += _w
IN_COLS = _off


def _head_norm(x, gain, ones_blk):
    w = x.shape[1]
    x2 = x * x
    hi = x2.astype(BF16)
    lo = (x2 - hi.astype(F32)).astype(BF16)
    m = ones_blk[:w, :w]
    ms = _dot(hi, m) + _dot(lo, m)
    return x * lax.rsqrt(ms + NORM_EPS) * gain


def _rotary(x, cos, sin_signed, half):
    w = x.shape[1]
    lane = _lane_iota(x.shape)
    fwd = pltpu.roll(x, w - half, 1)
    bwd = pltpu.roll(x, half, 1)
    partner = jnp.where((lane & half) == 0, fwd, bwd)
    return x * cos + partner * sin_signed


def _inproj_kernel(h_ref, gain_ref, w_ref, ones_ref, cos_ref, sin_ref, icos_ref, isin_ref,
                   aqg_ref, akg_ref, bqg_ref, bkg_ref,
                   aq_o, ak_o, av_o, bq_o, bk_o, bv_o, iq_o, ik_o, iw_o, ga_o, gb_o, *, iw_scale):
    h = h_ref[...]
    xn = (h * lax.rsqrt(jnp.mean(h * h, axis=-1, keepdims=True) + NORM_EPS) * gain_ref[...]).astype(BF16)

    def proj(name):
        a, b = _SEC[name]
        return _dot(xn, w_ref[:, a:b])

    ones_blk = ones_ref[...]
    cos = cos_ref[...]
    sin = sin_ref[...]
    aq_o[...] = _rotary(_head_norm(proj("aq"), aqg_ref[...], ones_blk), cos, sin, 32).astype(BF16)
    ak_o[...] = _rotary(_head_norm(proj("ak"), akg_ref[...], ones_blk), cos, sin, 32).astype(BF16)
    av_o[...] = proj("av").astype(BF16)
    bq_o[...] = _rotary(_head_norm(proj("bq"), bqg_ref[...], ones_blk), cos, sin, 32).astype(BF16)
    bk_o[...] = _rotary(_head_norm(proj("bk"), bkg_ref[...], ones_blk), cos[:, :LANES], sin[:, :LANES], 32).astype(BF16)
    bv = proj("bv")
    bv_o[...] = jnp.where(_lane_iota(bv.shape) < HEAD_DIM, bv, 1.0).astype(BF16)
    icos = icos_ref[...]
    isin = isin_ref[...]
    iq_o[...] = _rotary(proj("iq"), icos, isin, 16).astype(BF16)
    ik_o[...] = _rotary(proj("ik"), icos[:, :LANES], isin[:, :LANES], 16).astype(BF16)
    iw_o[...] = proj("iw") * iw_scale
    ga_o[...] = jax.nn.sigmoid(proj("ga")).astype(BF16)
    gb_o[...] = jax.nn.sigmoid(proj("gb")).astype(BF16)


def _rope_tables(t_len, dim, width_dim):
    inv = ROPE_BASE ** (-jnp.arange(0, dim, 2, dtype=F32) / dim)
    ang = jnp.arange(t_len, dtype=F32)[:, None] * inv[None, :]
    cos, sin = jnp.cos(ang), jnp.sin(ang)
    rest = width_dim - dim
    cos_h = jnp.concatenate([cos, cos, jnp.ones((t_len, rest), F32)], axis=1)
    sin_h = jnp.concatenate([-sin, sin, jnp.zeros((t_len, rest), F32)], axis=1)
    return cos_h, sin_h


def _input_projection(h2d, t_pad, mix_gain, w_in, aqg, akg, bqg, bkg):
    n_tok, d_model = h2d.shape
    tm = _pick_tile(t_pad, (384, 256, 128))
    per_seq = t_pad // tm

    sizes = (512, 512, 512, 512, 64, 64, 512, 64, IDX_HEADS, d_model, d_model)
    offs = [0]
    for s in sizes:
        offs.append(offs[-1] + s)
    cols = [w_in[:, offs[i]:offs[i + 1]] for i in range(len(sizes))]
    waq, wak, wav, wbq, wbk, wbv, wiq, wik, wiw, wga, wgb = cols
    wiw = jnp.pad(wiw, ((0, 0), (0, LANES - IDX_HEADS)))
    w_all = jnp.concatenate([waq, wak, wav, wbq, wbk, wbk, wbv, wbv, wiq, wik, wik, wiw, wga, wgb],
                            axis=1).astype(BF16)
    assert w_all.shape[1] == IN_COLS

    cos_h, sin_h = _rope_tables(t_pad, HEAD_DIM, HEAD_DIM)
    cos = jnp.tile(cos_h, (1, 8))
    sin = jnp.tile(sin_h, (1, 8))
    icos_h, isin_h = _rope_tables(t_pad, IDX_ROPE, HEAD_DIM)
    icos = jnp.tile(icos_h, (1, 8))
    isin = jnp.tile(isin_h, (1, 8))
    head_id = jnp.arange(512) // HEAD_DIM
    ones_blk = jnp.where(head_id[:, None] == head_id[None, :], 1.0 / HEAD_DIM, 0.0).astype(BF16)

    row = lambda i: (i, 0)
    fixed = lambda i: (0, 0)
    tab = lambda i: (i % per_seq, 0)
    widths = dict(aq=512, ak=512, av=512, bq=512, bk=128, bv=128, iq=512, ik=128, iw=128, ga=d_model, gb=d_model)
    out_dtypes = dict(iw=F32)
    names = ("aq", "ak", "av", "bq", "bk", "bv", "iq", "ik", "iw", "ga", "gb")
    out_shape = tuple(jax.ShapeDtypeStruct((n_tok, widths[k]), out_dtypes.get(k, BF16)) for k in names)
    out_specs = tuple(pl.BlockSpec((tm, widths[k]), row) for k in names)
    in_specs = [
        pl.BlockSpec((tm, d_model), row),
        pl.BlockSpec((1, d_model), fixed),
        pl.BlockSpec((d_model, IN_COLS), fixed),
        pl.BlockSpec((512, 512), fixed),
        pl.BlockSpec((tm, 512), tab), pl.BlockSpec((tm, 512), tab),
        pl.BlockSpec((tm, 512), tab), pl.BlockSpec((tm, 512), tab),
        pl.BlockSpec((1, 512), fixed), pl.BlockSpec((1, 512), fixed),
        pl.BlockSpec((1, 512), fixed), pl.BlockSpec((1, LANES), fixed),
    ]
    iw_scale = IDX_HEADS ** -0.5 * HEAD_DIM ** -0.5
    outs = pl.pallas_call(
        functools.partial(_inproj_kernel, iw_scale=iw_scale),
        grid=(n_tok // tm,),
        in_specs=in_specs, out_specs=out_specs, out_shape=out_shape,
        compiler_params=_params(("parallel",)),
        name="inproj",
    )(h2d, mix_gain.reshape(1, d_model), w_all, ones_blk, cos, sin, icos, isin, aqg, akg, bqg, bkg)
    return dict(zip(names, outs))


def _stack_heads(x):
    rows, width = x.shape
    lane = _lane_iota((rows, LANES))
    zero = jnp.zeros((rows, LANES), x.dtype)
    groups = []
    for c in range(width // LANES):
        blk = x[:, c * LANES:(c + 1) * LANES]
        groups.append(jnp.where(lane < HEAD_DIM, blk, zero))
        groups.append(jnp.where(lane >= HEAD_DIM, blk, zero))
    return jnp.concatenate(groups, axis=0)


def _softmax_step(s, valid, v, m, l, acc):
    s = jnp.where(valid, s, MASKED_SCORE)
    m_new = jnp.maximum(m, jnp.max(s, axis=1, keepdims=True))
    alpha = jnp.exp2(m - m_new)
    p = jnp.exp2(s - m_new)
    l_new = alpha * l + jnp.sum(p, axis=1, keepdims=True)
    acc_new = alpha * acc + _dot(p.astype(BF16), v)
    return m_new, l_new, acc_new


def _diff_kernel(q_ref, k_ref, v_ref, lq1_ref, lk1_ref, lq2_ref, lk2_ref, gain_ref, o_ref, *,
                 tq, tk, lambda_init):
    i = pl.program_id(2)
    qs = _stack_heads(q_ref[...])
    rows = 2 * tq
    qpos = i * tq + lax.broadcasted_iota(I32, (rows, tk), 0) % tq
    kcol = _lane_iota((rows, tk))

    def body(j, carry):
        m, l, acc = carry
        start = pl.multiple_of(j * tk, tk)
        k = k_ref[pl.ds(start, tk), :]
        v = v_ref[pl.ds(start, tk), :]
        s = _dot_nt(qs, k)
        valid = (kcol + j * tk) <= qpos
        return _softmax_step(s, valid, v, m, l, acc)

    n_chunks = ((i + 1) * tq + tk - 1) // tk
    init = (jnp.full((rows, 1), RUNNING_MAX_FLOOR, F32), jnp.zeros((rows, 1), F32), jnp.zeros((rows, LANES), F32))
    m, l, acc = lax.fori_loop(0, n_chunks, body, init)
    o = acc / l
    lam = (jnp.exp(jnp.sum(lq1_ref[...] * lk1_ref[...], axis=-1, keepdims=True))
           - jnp.exp(jnp.sum(lq2_ref[...] * lk2_ref[...], axis=-1, keepdims=True)) + lambda_init)
    d = o[:tq] - lam * o[tq:]
    d = d * lax.rsqrt(jnp.mean(d * d, axis=-1, keepdims=True) + NORM_EPS) * gain_ref[...]
    o_ref[...] = (d * (1.0 - lambda_init)).astype(BF16)


def _diff_attention(aq, ak, av, bsz, t_pad, lq1, lk1, lq2, lk2, subln_gain, lambda_init):
    tq = _pick_tile(t_pad, (384, 256, 128))
    tk = tq
    nq = t_pad // tq
    qmap = lambda b, h, i: (b * nq + i, h)
    kvmap = lambda b, h, i: (b, h)
    fixed = lambda b, h, i: (0, 0)
    vec = pl.BlockSpec((1, HEAD_DIM), fixed)
    return pl.pallas_call(
        functools.partial(_diff_kernel, tq=tq, tk=tk, lambda_init=lambda_init),
        grid=(bsz, A_HEADS, nq),
        in_specs=[pl.BlockSpec((tq, LANES), qmap), pl.BlockSpec((t_pad, LANES), kvmap),
                  pl.BlockSpec((t_pad, LANES), kvmap), vec, vec, vec, vec, pl.BlockSpec((1, LANES), fixed)],
        out_specs=pl.BlockSpec((tq, LANES), qmap),
        out_shape=jax.ShapeDtypeStruct(aq.shape, BF16),
        compiler_params=_params(("parallel", "parallel", "arbitrary")),
        name="diff_attn",
    )(aq, ak, av, lq1, lk1, lq2, lk2, subln_gain)


def _dsa_kernel(iq_ref, iw_ref, ik_ref, bq_ref, bk_ref, bv_ref, tri_ref, o_ref,
                key_ref, iqs_ref, qs_ref, m_ref, acc_ref, *, tq, tk, k_sel):
    i = pl.program_id(1)
    n_chunks = ((i + 1) * tq + tk - 1) // tk
    lane_blocks = tk // LANES
    qpos_col = i * tq + lax.broadcasted_iota(I32, (tq, 1), 0)
    kcol = _lane_iota((tq, tk))

    iqs_ref[...] = _stack_heads(iq_ref[...])
    iw = iw_ref[...]

    def score_body(j, _):
        start = pl.multiple_of(j * tk, tk)
        ik = ik_ref[pl.ds(start, tk), :]
        score = jnp.zeros((tq, tk), F32)
        for h in range(IDX_HEADS):
            logits = _dot_nt(iqs_ref[h * tq:(h + 1) * tq, :], ik)
            score = score + jnp.maximum(logits, 0.0) * iw[:, h:h + 1]
        score = jnp.where((kcol + j * tk) <= qpos_col, score, -jnp.inf)
        bits = lax.bitcast_convert_type(score, I32)
        key_ref[:, pl.ds(start, tk)] = jnp.where(bits < 0, (bits ^ 0x7FFFFFFF) + 1, bits)
        return 0

    lax.fori_loop(0, n_chunks, score_body, 0)

    def count_ge(cand_col):
        counts = []
        for r0 in range(0, tq, SEQ_BLOCK):
            rows = slice(r0, r0 + SEQ_BLOCK)
            cand = jnp.broadcast_to(cand_col[rows], (SEQ_BLOCK, LANES))

            def body(j, acc, rows=rows, cand=cand):
                start = pl.multiple_of(j * tk, tk)
                for b in range(lane_blocks):
                    blk = key_ref[rows, pl.ds(start + b * LANES, LANES)]
                    acc = acc + jnp.where(blk >= cand, 1.0, 0.0)
                return acc

            counts.append(lax.fori_loop(0, n_chunks, body, jnp.zeros((SEQ_BLOCK, LANES), F32)))
        return jnp.sum(jnp.concatenate(counts, axis=0), axis=1, keepdims=True)

    kf = float(k_sel)
    thr = jnp.where(count_ge(jnp.zeros((tq, 1), I32)) >= kf, 0, INT_MIN).astype(I32)

    def bit_body(t, thr):
        cand = thr + jnp.left_shift(jnp.int32(1), 30 - t)
        return jnp.where(count_ge(cand) >= kf, cand, thr)

    thr = lax.fori_loop(0, 31, bit_body, thr)
    ties_kept = kf - count_ge(thr + 1)

    qs_ref[...] = _stack_heads(bq_ref[...])
    m_ref[...] = jnp.full(m_ref.shape, RUNNING_MAX_FLOOR, F32)
    acc_ref[...] = jnp.zeros(acc_ref.shape, F32)

    def attn_body(j, ties_before):
        start = pl.multiple_of(j * tk, tk)
        key = key_ref[:, pl.ds(start, tk)]
        tie = key == thr
        tie_count = ties_before + _dot(jnp.where(tie, 1.0, 0.0).astype(BF16), tri_ref[...])
        sel = ((key > thr) | (tie & (tie_count <= ties_kept))) & ((kcol + j * tk) <= qpos_col)
        bias = jnp.where(sel, 0.0, MASKED_SCORE)
        bk = bk_ref[pl.ds(start, tk), :]
        bv = bv_ref[pl.ds(start, tk), :]
        for h in range(B_HEADS):
            rows = slice(h * tq, (h + 1) * tq)
            s = _dot_nt(qs_ref[rows, :], bk) + bias
            m_prev = m_ref[rows, :]
            m_new = jnp.maximum(m_prev, jnp.max(s, axis=1, keepdims=True))
            alpha = jnp.exp2(m_prev - m_new)
            p = jnp.exp2(s - jnp.concatenate([m_new] * lane_blocks, axis=1))
            acc_ref[rows, :] = alpha * acc_ref[rows, :] + _dot(p.astype(BF16), bv)
            m_ref[rows, :] = m_new
        return tie_count[:, tk - 1:tk]

    lax.fori_loop(0, n_chunks, attn_body, jnp.zeros((tq, 1), F32))
    lane = _lane_iota((tq, LANES))
    for c in range(B_HEADS // 2):
        even = acc_ref[(2 * c) * tq:(2 * c + 1) * tq, :]
        odd = acc_ref[(2 * c + 1) * tq:(2 * c + 2) * tq, :]
        o_ref[:, c * LANES:(c + 1) * LANES] = jnp.where(
            lane < HEAD_DIM, even / pltpu.roll(even, HEAD_DIM, 1), pltpu.roll(odd, HEAD_DIM, 1) / odd).astype(BF16)


def _dsa_attention(iq, iw, ik, bq, bk, bv, bsz, t_pad, k_sel):
    tq = _pick_tile(t_pad, (384, 256, 128))
    tk = tq
    nq = t_pad // tq
    qmap = lambda b, i: (b * nq + i, 0)
    kvmap = lambda b, i: (b, 0)
    tri = (jnp.arange(tk)[:, None] <= jnp.arange(tk)[None, :]).astype(BF16)
    stacked = pltpu.VMEM((B_HEADS * tq, LANES), BF16)
    stats = pltpu.VMEM((B_HEADS * tq, LANES), F32)
    return pl.pallas_call(
        functools.partial(_dsa_kernel, tq=tq, tk=tk, k_sel=k_sel),
        grid=(bsz, nq),
        in_specs=[pl.BlockSpec((tq, 512), qmap), pl.BlockSpec((tq, LANES), qmap), pl.BlockSpec((t_pad, LANES), kvmap),
                  pl.BlockSpec((tq, 512), qmap), pl.BlockSpec((t_pad, LANES), kvmap), pl.BlockSpec((t_pad, LANES), kvmap),
                  pl.BlockSpec((tk, tk), lambda b, i: (0, 0))],
        out_specs=pl.BlockSpec((tq, 512), qmap),
        out_shape=jax.ShapeDtypeStruct(bq.shape, BF16),
        scratch_shapes=[pltpu.VMEM((tq, t_pad), I32), stacked, stacked, stats, stats],
        compiler_params=_params(("parallel", "arbitrary")),
        name="dsa_attn",
    )(iq, iw, ik, bq, bk, bv, tri)


def _merge_kernel(h_ref, a_ref, b_ref, ga_ref, gb_ref, wa_ref, wb_ref, wo_ref, gain_ref, wq_ref,
                  h1_o, hn_o, pq_o):
    ya = _dot(a_ref[...], wa_ref[...])
    yb = _dot(b_ref[...], wb_ref[...])
    mixed = ga_ref[...].astype(F32) * ya + gb_ref[...].astype(F32) * yb
    h1 = h_ref[...] + _dot(mixed.astype(BF16), wo_ref[...])
    h1_o[...] = h1
    hn = (h1 * lax.rsqrt(jnp.mean(h1 * h1, axis=-1, keepdims=True) + NORM_EPS) * gain_ref[...]).astype(BF16)
    hn_o[...] = hn
    pq_o[...] = _dot(hn, wq_ref[...]).astype(BF16)


def _merge(h2d, a_out, b_out, ga, gb, w_a, w_b, w_o, ffn_gain, w_query):
    n_tok, d_model = h2d.shape
    tm = _pick_tile(n_tok, (256, 128))
    row = lambda i: (i, 0)
    fixed = lambda i: (0, 0)
    pw = w_query.shape[1]
    return pl.pallas_call(
        _merge_kernel,
        grid=(n_tok // tm,),
        in_specs=[pl.BlockSpec((tm, d_model), row), pl.BlockSpec((tm, 512), row), pl.BlockSpec((tm, 512), row),
                  pl.BlockSpec((tm, d_model), row), pl.BlockSpec((tm, d_model), row),
                  pl.BlockSpec((512, d_model), fixed), pl.BlockSpec((512, d_model), fixed),
                  pl.BlockSpec((d_model, d_model), fixed), pl.BlockSpec((1, d_model), fixed),
                  pl.BlockSpec((d_model, pw), fixed)],
        out_specs=(pl.BlockSpec((tm, d_model), row), pl.BlockSpec((tm, d_model), row), pl.BlockSpec((tm, pw), row)),
        out_shape=(jax.ShapeDtypeStruct((n_tok, d_model), F32), jax.ShapeDtypeStruct((n_tok, d_model), BF16),
                   jax.ShapeDtypeStruct((n_tok, pw), BF16)),
        compiler_params=_params(("parallel",)),
        name="merge",
    )(h2d, a_out, b_out, ga, gb, w_a.astype(BF16), w_b.astype(BF16), w_o.astype(BF16),
      ffn_gain.reshape(1, d_model), w_query.astype(BF16))


def _top_values(x, count):
    vals = []
    cur = x
    rank = jnp.full(x.shape, float(count), F32)
    for it in range(count):
        m = jnp.max(cur, axis=0, keepdims=True)
        vals.append(m)
        hit = cur == m
        rank = jnp.where(hit, float(it), rank)
        cur = jnp.where(hit, -jnp.inf, cur)
    return jnp.concatenate(vals, axis=0), rank


def _pair_sum_candidates(v1, v2):
    k = v1.shape[0]
    pieces = []
    for a in range(k // 2):
        nb = min(k, -(-(k // (a + 1)) // 8) * 8)
        pieces.append(v1[a:a + 1] + v2[:nb])
    pieces.append(v1[k // 2:] + v2[0:1])
    return jnp.concatenate(pieces, axis=0)


def _peer_routing(pq_ref, sk_ref, cut_ref, e1_ref, rank_ref, e2_ref, n_keys):
    def head_body(h, _):
        lo = pl.multiple_of(h * LANES, LANES)
        st = _dot_nt(sk_ref[h], pq_ref[:, pl.ds(lo, LANES)])
        for c in range(st.shape[1] // LANES):
            cols = slice(c * LANES, (c + 1) * LANES)
            s1 = st[:n_keys, cols]
            s2 = st[n_keys:, cols]
            v1, _ = _top_values(s1, PEER_TOPK)
            v2, rank = _top_values(s2, PEER_TOPK)
            cand = _pair_sum_candidates(v1, v2)
            thr = _top_values(cand, PEER_TOPK)[0][PEER_TOPK - 1:PEER_TOPK]
            top = v1[0:1] + v2[0:1]
            z = jnp.sum(jnp.where(cand >= thr, jnp.exp(cand - top), 0.0), axis=0, keepdims=True)
            cut = jnp.zeros_like(s1)
            for b in range(PEER_TOPK):
                cut = cut + jnp.where(s1 + v2[b:b + 1] >= thr, 1.0, 0.0)
            cut_ref[h, :, cols] = cut
            e1_ref[h, :, cols] = jnp.exp(s1 - v1[0:1])
            rank_ref[h, :, cols] = rank.astype(BF16)
            e2_ref[h, :, cols] = (jnp.exp(s2 - v2[0:1]) / z).astype(BF16)
        return 0

    lax.fori_loop(0, PEER_HEADS, head_body, 0)


def _gelu_exact(x):
    return 0.5 * x * (1.0 + lax.erf(x * (1.0 / math.sqrt(2.0))))


def _peer_kernel(h1_ref, hn_ref, pq_ref, sk_ref, u_ref, vt_ref, o_ref,
                 cut_ref, e1_ref, rank_ref, e2_ref, p_ref, acc_ref, *, te, grp, n_keys):
    j = pl.program_id(1)
    tm = hn_ref.shape[0]
    rows_per_step = te // n_keys
    rows_per_grp = grp // n_keys
    n_groups = te // grp

    @pl.when(j == 0)
    def _():
        _peer_routing(pq_ref, sk_ref, cut_ref, e1_ref, rank_ref, e2_ref, n_keys)
        acc_ref[...] = jnp.zeros_like(acc_ref)

    hn = hn_ref[...]

    def activations(g):
        return _dot_nt(u_ref[g * grp:(g + 1) * grp, :], hn)

    def weighted(g, act):
        for c in range(tm // LANES):
            cols = slice(c * LANES, (c + 1) * LANES)
            tiles = {}
            for r in range(rows_per_grp):
                row = g * rows_per_grp + r
                k = row % 8
                w = jnp.zeros((n_keys, LANES), BF16)
                for h in range(PEER_HEADS):
                    if (h, row // 8) not in tiles:
                        row8 = pl.ds(pl.multiple_of(j * rows_per_step + (row // 8) * 8, 8), 8)
                        tiles[(h, row // 8)] = (cut_ref[h, row8, cols], e1_ref[h, row8, cols])
                    cut8, e18 = tiles[(h, row // 8)]
                    cut_b = jnp.broadcast_to(cut8[k:k + 1].astype(BF16), (n_keys, LANES))
                    e1_b = jnp.broadcast_to(e18[k:k + 1].astype(BF16), (n_keys, LANES))
                    hit = rank_ref[h, :, cols] < cut_b
                    w = w + jnp.where(hit, e2_ref[h, :, cols], jnp.zeros((), BF16)) * e1_b
                a = act[r * n_keys:(r + 1) * n_keys, cols]
                p_ref[g % 2, r * n_keys:(r + 1) * n_keys, cols] = w * _gelu_exact(a).astype(BF16)

    def accumulate(g):
        acc_ref[...] += _dot(vt_ref[:, g * grp:(g + 1) * grp], p_ref[g % 2])

    act = activations(0)
    for g in range(n_groups):
        nxt = activations(g + 1) if g + 1 < n_groups else None
        weighted(g, act)
        if g > 0:
            accumulate(g - 1)
        act = nxt
    accumulate(n_groups - 1)

    @pl.when(j == pl.num_programs(1) - 1)
    def _():
        o_ref[...] = h1_ref[...] + acc_ref[...].T


def _peer(h1, hn, pq, sub_keys, peer_u, peer_v):
    n_tok, d_model = h1.shape
    n_heads, _, n_keys, half = sub_keys.shape
    n_experts = peer_u.shape[0]
    assert n_heads == PEER_HEADS and n_keys == LANES and 2 * half == LANES and n_experts == n_keys * n_keys
    tm = _pick_tile(n_tok, (512, 384, 256, 128))
    te = 2048
    grp = 256
    zeros = jnp.zeros((n_heads, n_keys, half), sub_keys.dtype)
    sk = jnp.concatenate([jnp.concatenate([sub_keys[:, 0], zeros], axis=2),
                          jnp.concatenate([zeros, sub_keys[:, 1]], axis=2)], axis=1).astype(BF16)
    u = peer_u.astype(BF16)
    vt = peer_v.astype(BF16).T
    row = lambda i, j: (i, 0)
    route_f32 = pltpu.VMEM((PEER_HEADS, n_keys, tm), F32)
    route_bf16 = pltpu.VMEM((PEER_HEADS, n_keys, tm), BF16)
    return pl.pallas_call(
        functools.partial(_peer_kernel, te=te, grp=grp, n_keys=n_keys),
        grid=(n_tok // tm, n_experts // te),
        in_specs=[pl.BlockSpec((tm, d_model), row), pl.BlockSpec((tm, d_model), row),
                  pl.BlockSpec((tm, PEER_HEADS * LANES), row),
                  pl.BlockSpec((n_heads, 2 * n_keys, LANES), lambda i, j: (0, 0, 0)),
                  pl.BlockSpec((te, d_model), lambda i, j: (j, 0)),
                  pl.BlockSpec((d_model, te), lambda i, j: (0, j))],
        out_specs=pl.BlockSpec((tm, d_model), row),
        out_shape=jax.ShapeDtypeStruct((n_tok, d_model), F32),
        scratch_shapes=[route_f32, route_f32, route_bf16, route_bf16,
                        pltpu.VMEM((2, grp, tm), BF16), pltpu.VMEM((d_model, tm), F32)],
        compiler_params=_params(("parallel", "arbitrary")),
        name="peer",
    )(h1, hn, pq, sk, u, vt)


def kernel(x, meta_tokens, mix_norm_gain, w_in, a_q_norm_gain, a_k_norm_gain, a_lambda_q1, a_lambda_k1, a_lambda_q2, a_lambda_k2, a_subln_gain, w_branch_a, b_q_norm_gain, b_k_norm_gain, w_branch_b, w_out, ffn_norm_gain, peer_w_query, peer_sub_keys, peer_u, peer_v):
    bsz, seq, d_model = x.shape
    depth = w_in.shape[0]
    t_real = N_META_TOKENS + seq
    t_pad = ((t_real + SEQ_BLOCK - 1) // SEQ_BLOCK) * SEQ_BLOCK
    k_sel = min(TOPK_LIMIT, seq // 4)

    meta = jnp.broadcast_to(meta_tokens.astype(x.dtype)[None], (bsz, N_META_TOKENS, d_model))
    pad = jnp.zeros((bsz, t_pad - t_real, d_model), x.dtype)
    h = jnp.concatenate([meta, x, pad], axis=1).reshape(bsz * t_pad, d_model)

    q_scale = HEAD_DIM ** -0.5 * LOG2_E
    for layer in range(depth):
        lambda_init = 0.8 - 0.6 * math.exp(-0.3 * layer)
        aqg = (jnp.tile(a_q_norm_gain[layer], 8) * q_scale).reshape(1, 512)
        akg = jnp.tile(a_k_norm_gain[layer], 8).reshape(1, 512)
        bqg = (jnp.tile(b_q_norm_gain[layer], 8) * q_scale).reshape(1, 512)
        bkg = jnp.tile(b_k_norm_gain[layer], 2).reshape(1, LANES)
        p = _input_projection(h, t_pad, mix_norm_gain[layer], w_in[layer], aqg, akg, bqg, bkg)
        vec = lambda a: a[layer].astype(F32).reshape(1, HEAD_DIM)
        a_out = _diff_attention(p["aq"], p["ak"], p["av"], bsz, t_pad, vec(a_lambda_q1), vec(a_lambda_k1),
                                vec(a_lambda_q2), vec(a_lambda_k2), a_subln_gain[layer].reshape(1, LANES), lambda_init)
        b_out = _dsa_attention(p["iq"], p["iw"], p["ik"], p["bq"], p["bk"], p["bv"], bsz, t_pad, k_sel)
        h1, hn, pq = _merge(h, a_out, b_out, p["ga"], p["gb"], w_branch_a[layer], w_branch_b[layer], w_out[layer],
                            ffn_norm_gain[layer], peer_w_query[layer])
        h = _peer(h1, hn, pq, peer_sub_keys[layer], peer_u[layer], peer_v[layer])

    return h.reshape(bsz, t_pad, d_model)[:, N_META_TOKENS:N_META_TOKENS + seq]
```

```python
import functools
import math

import jax
import jax.numpy as jnp
from jax import lax
from jax.experimental import pallas as pl
from jax.experimental.pallas import tpu as pltpu

F32 = jnp.float32
BF16 = jnp.bfloat16
I32 = jnp.int32

N_META_TOKENS = 16
SEQ_BLOCK = 128
ROPE_BASE = 10000.0
NORM_EPS = 1e-6
HEAD_DIM = 64
LANES = 128
A_HEADS = 4
B_HEADS = 8
IDX_HEADS = 8
IDX_ROPE = 32
TOPK_LIMIT = 256
PEER_HEADS = 8
PEER_TOPK = 16
MASKED_SCORE = -1e30
RUNNING_MAX_FLOOR = -1e29
INT_MIN = -(2 ** 31)
LOG2_E = math.log2(math.e)
VMEM_LIMIT = 56 * 1024 * 1024


def _pick_tile(n, candidates):
    for c in candidates:
        if n % c == 0:
            return c
    raise ValueError(f"no tile in {candidates} divides {n}")


def _params(semantics):
    return pltpu.CompilerParams(dimension_semantics=semantics, vmem_limit_bytes=VMEM_LIMIT)


def _dot(a, b):
    return jnp.dot(a, b, preferred_element_type=F32)


def _dot_nt(a, b):
    return lax.dot_general(a, b, (((1,), (1,)), ((), ())), preferred_element_type=F32)


def _lane_iota(shape):
    return lax.broadcasted_iota(I32, shape, 1)


_SEC = {}
_off = 0
for _name, _w in (("aq", 512), ("ak", 512), ("av", 512), ("bq", 512), ("bk", 128), ("bv", 128),
                  ("iq", 512), ("ik", 128), ("iw", 128), ("ga", 1024), ("gb", 1024)):
    _SEC[_name] = (_off, _off + _w)
    _off += _w
IN_COLS = _off


def _head_norm(x, gain, ones_blk):
    w = x.shape[1]
    x2 = x * x
    hi = x2.astype(BF16)
    lo = (x2 - hi.astype(F32)).astype(BF16)
    m = ones_blk[:w, :w]
    ms = _dot(hi, m) + _dot(lo, m)
    return x * lax.rsqrt(ms + NORM_EPS) * gain


def _rotary(x, cos, sin_signed, half):
    w = x.shape[1]
    lane = _lane_iota(x.shape)
    fwd = pltpu.roll(x, w - half, 1)
    bwd = pltpu.roll(x, half, 1)
    partner = jnp.where((lane & half) == 0, fwd, bwd)
    return x * cos + partner * sin_signed


def _inproj_kernel(h_ref, gain_ref, w_ref, ones_ref, cos_ref, sin_ref, icos_ref, isin_ref,
                   aqg_ref, akg_ref, bqg_ref, bkg_ref,
                   aq_o, ak_o, av_o, bq_o, bk_o, bv_o, iq_o, ik_o, iw_o, ga_o, gb_o, *, iw_scale):
    h = h_ref[...]
    xn = (h * lax.rsqrt(jnp.mean(h * h, axis=-1, keepdims=True) + NORM_EPS) * gain_ref[...]).astype(BF16)

    def proj(name):
        a, b = _SEC[name]
        return _dot(xn, w_ref[:, a:b])

    ones_blk = ones_ref[...]
    cos = cos_ref[...]
    sin = sin_ref[...]
    aq_o[...] = _rotary(_head_norm(proj("aq"), aqg_ref[...], ones_blk), cos, sin, 32).astype(BF16)
    ak_o[...] = _rotary(_head_norm(proj("ak"), akg_ref[...], ones_blk), cos, sin, 32).astype(BF16)
    av_o[...] = proj("av").astype(BF16)
    bq_o[...] = _rotary(_head_norm(proj("bq"), bqg_ref[...], ones_blk), cos, sin, 32).astype(BF16)
    bk_o[...] = _rotary(_head_norm(proj("bk"), bkg_ref[...], ones_blk), cos[:, :LANES], sin[:, :LANES], 32).astype(BF16)
    bv = proj("bv")
    bv_o[...] = jnp.where(_lane_iota(bv.shape) < HEAD_DIM, bv, 1.0).astype(BF16)
    icos = icos_ref[...]
    isin = isin_ref[...]
    iq_o[...] = _rotary(proj("iq"), icos, isin, 16).astype(BF16)
    ik_o[...] = _rotary(proj("ik"), icos[:, :LANES], isin[:, :LANES], 16).astype(BF16)
    iw_o[...] = proj("iw") * iw_scale
    ga_o[...] = jax.nn.sigmoid(proj("ga")).astype(BF16)
    gb_o[...] = jax.nn.sigmoid(proj("gb")).astype(BF16)


def _rope_tables(t_len, dim, width_dim):
    inv = ROPE_BASE ** (-jnp.arange(0, dim, 2, dtype=F32) / dim)
    ang = jnp.arange(t_len, dtype=F32)[:, None] * inv[None, :]
    cos, sin = jnp.cos(ang), jnp.sin(ang)
    rest = width_dim - dim
    cos_h = jnp.concatenate([cos, cos, jnp.ones((t_len, rest), F32)], axis=1)
    sin_h = jnp.concatenate([-sin, sin, jnp.zeros((t_len, rest), F32)], axis=1)
    return cos_h, sin_h


def _input_projection(h2d, t_pad, mix_gain, w_in, aqg, akg, bqg, bkg):
    n_tok, d_model = h2d.shape
    tm = _pick_tile(t_pad, (384, 256, 128))
    per_seq = t_pad // tm

    sizes = (512, 512, 512, 512, 64, 64, 512, 64, IDX_HEADS, d_model, d_model)
    offs = [0]
    for s in sizes:
        offs.append(offs[-1] + s)
    cols = [w_in[:, offs[i]:offs[i + 1]] for i in range(len(sizes))]
    waq, wak, wav, wbq, wbk, wbv, wiq, wik, wiw, wga, wgb = cols
    wiw = jnp.pad(wiw, ((0, 0), (0, LANES - IDX_HEADS)))
    w_all = jnp.concatenate([waq, wak, wav, wbq, wbk, wbk, wbv, wbv, wiq, wik, wik, wiw, wga, wgb],
                            axis=1).astype(BF16)
    assert w_all.shape[1] == IN_COLS

    cos_h, sin_h = _rope_tables(t_pad, HEAD_DIM, HEAD_DIM)
    cos = jnp.tile(cos_h, (1, 8))
    sin = jnp.tile(sin_h, (1, 8))
    icos_h, isin_h = _rope_tables(t_pad, IDX_ROPE, HEAD_DIM)
    icos = jnp.tile(icos_h, (1, 8))
    isin = jnp.tile(isin_h, (1, 8))
    head_id = jnp.arange(512) // HEAD_DIM
    ones_blk = jnp.where(head_id[:, None] == head_id[None, :], 1.0 / HEAD_DIM, 0.0).astype(BF16)

    row = lambda i: (i, 0)
    fixed = lambda i: (0, 0)
    tab = lambda i: (i % per_seq, 0)
    widths = dict(aq=512, ak=512, av=512, bq=512, bk=128, bv=128, iq=512, ik=128, iw=128, ga=d_model, gb=d_model)
    out_dtypes = dict(iw=F32)
    names = ("aq", "ak", "av", "bq", "bk", "bv", "iq", "ik", "iw", "ga", "gb")
    out_shape = tuple(jax.ShapeDtypeStruct((n_tok, widths[k]), out_dtypes.get(k, BF16)) for k in names)
    out_specs = tuple(pl.BlockSpec((tm, widths[k]), row) for k in names)
    in_specs = [
        pl.BlockSpec((tm, d_model), row),
        pl.BlockSpec((1, d_model), fixed),
        pl.BlockSpec((d_model, IN_COLS), fixed),
        pl.BlockSpec((512, 512), fixed),
        pl.BlockSpec((tm, 512), tab), pl.BlockSpec((tm, 512), tab),
        pl.BlockSpec((tm, 512), tab), pl.BlockSpec((tm, 512), tab),
        pl.BlockSpec((1, 512), fixed), pl.BlockSpec((1, 512), fixed),
        pl.BlockSpec((1, 512), fixed), pl.BlockSpec((1, LANES), fixed),
    ]
    iw_scale = IDX_HEADS ** -0.5 * HEAD_DIM ** -0.5
    outs = pl.pallas_call(
        functools.partial(_inproj_kernel, iw_scale=iw_scale),
        grid=(n_tok // tm,),
        in_specs=in_specs, out_specs=out_specs, out_shape=out_shape,
        compiler_params=_params(("parallel",)),
        name="inproj",
    )(h2d, mix_gain.reshape(1, d_model), w_all, ones_blk, cos, sin, icos, isin, aqg, akg, bqg, bkg)
    return dict(zip(names, outs))


def _stack_heads(x):
    rows, width = x.shape
    lane = _lane_iota((rows, LANES))
    zero = jnp.zeros((rows, LANES), x.dtype)
    groups = []
    for c in range(width // LANES):
        blk = x[:, c * LANES:(c + 1) * LANES]
        groups.append(jnp.where(lane < HEAD_DIM, blk, zero))
        groups.append(jnp.where(lane >= HEAD_DIM, blk, zero))
    return jnp.concatenate(groups, axis=0)


def _softmax_step(s, valid, v, m, l, acc):
    s = jnp.where(valid, s, MASKED_SCORE)
    m_new = jnp.maximum(m, jnp.max(s, axis=1, keepdims=True))
    alpha = jnp.exp2(m - m_new)
    p = jnp.exp2(s - m_new)
    l_new = alpha * l + jnp.sum(p, axis=1, keepdims=True)
    acc_new = alpha * acc + _dot(p.astype(BF16), v)
    return m_new, l_new, acc_new


def _diff_kernel(q_ref, k_ref, v_ref, lq1_ref, lk1_ref, lq2_ref, lk2_ref, gain_ref, o_ref, *,
                 tq, tk, lambda_init):
    i = pl.program_id(2)
    qs = _stack_heads(q_ref[...])
    rows = 2 * tq
    qpos = i * tq + lax.broadcasted_iota(I32, (rows, tk), 0) % tq
    kcol = _lane_iota((rows, tk))

    def body(j, carry):
        m, l, acc = carry
        start = pl.multiple_of(j * tk, tk)
        k = k_ref[pl.ds(start, tk), :]
        v = v_ref[pl.ds(start, tk), :]
        s = _dot_nt(qs, k)
        valid = (kcol + j * tk) <= qpos
        return _softmax_step(s, valid, v, m, l, acc)

    n_chunks = ((i + 1) * tq + tk - 1) // tk
    init = (jnp.full((rows, 1), RUNNING_MAX_FLOOR, F32), jnp.zeros((rows, 1), F32), jnp.zeros((rows, LANES), F32))
    m, l, acc = lax.fori_loop(0, n_chunks, body, init)
    o = acc / l
    lam = (jnp.exp(jnp.sum(lq1_ref[...] * lk1_ref[...], axis=-1, keepdims=True))
           - jnp.exp(jnp.sum(lq2_ref[...] * lk2_ref[...], axis=-1, keepdims=True)) + lambda_init)
    d = o[:tq] - lam * o[tq:]
    d = d * lax.rsqrt(jnp.mean(d * d, axis=-1, keepdims=True) + NORM_EPS) * gain_ref[...]
    o_ref[...] = (d * (1.0 - lambda_init)).astype(BF16)


def _diff_attention(aq, ak, av, bsz, t_pad, lq1, lk1, lq2, lk2, subln_gain, lambda_init):
    tq = _pick_tile(t_pad, (384, 256, 128))
    tk = tq
    nq = t_pad // tq
    qmap = lambda b, h, i: (b * nq + i, h)
    kvmap = lambda b, h, i: (b, h)
    fixed = lambda b, h, i: (0, 0)
    vec = pl.BlockSpec((1, HEAD_DIM), fixed)
    return pl.pallas_call(
        functools.partial(_diff_kernel, tq=tq, tk=tk, lambda_init=lambda_init),
        grid=(bsz, A_HEADS, nq),
        in_specs=[pl.BlockSpec((tq, LANES), qmap), pl.BlockSpec((t_pad, LANES), kvmap),
                  pl.BlockSpec((t_pad, LANES), kvmap), vec, vec, vec, vec, pl.BlockSpec((1, LANES), fixed)],
        out_specs=pl.BlockSpec((tq, LANES), qmap),
        out_shape=jax.ShapeDtypeStruct(aq.shape, BF16),
        compiler_params=_params(("parallel", "parallel", "arbitrary")),
        name="diff_attn",
    )(aq, ak, av, lq1, lk1, lq2, lk2, subln_gain)


def _dsa_kernel(iq_ref, iw_ref, ik_ref, bq_ref, bk_ref, bv_ref, tri_ref, o_ref,
                key_ref, iqs_ref, qs_ref, s_ref, p_ref, bias_ref, m_ref, alpha_ref, acc_ref, *, tq, tk, k_sel):
    i = pl.program_id(1)
    n_chunks = ((i + 1) * tq + tk - 1) // tk
    lane_blocks = tk // LANES
    qpos_col = i * tq + lax.broadcasted_iota(I32, (tq, 1), 0)
    kcol = _lane_iota((tq, tk))

    iqs_ref[...] = _stack_heads(iq_ref[...])
    iw = iw_ref[...]

    def score_body(j, _):
        start = pl.multiple_of(j * tk, tk)
        s_ref[...] = _dot_nt(iqs_ref[...], ik_ref[pl.ds(start, tk), :])
        for r0 in range(0, tq, SEQ_BLOCK):
            rows = slice(r0, r0 + SEQ_BLOCK)
            score = jnp.zeros((SEQ_BLOCK, tk), F32)
            for h in range(IDX_HEADS):
                logits = s_ref[h * tq + r0:h * tq + r0 + SEQ_BLOCK, :]
                score = score + jnp.maximum(logits, 0.0) * iw_ref[rows, h:h + 1]
            qpos_blk = i * tq + r0 + lax.broadcasted_iota(I32, (SEQ_BLOCK, 1), 0)
            score = jnp.where((_lane_iota((SEQ_BLOCK, tk)) + j * tk) <= qpos_blk, score, -jnp.inf)
            bits = lax.bitcast_convert_type(score, I32)
            key_ref[rows, pl.ds(start, tk)] = jnp.where(bits < 0, (bits ^ 0x7FFFFFFF) + 1, bits)
        return 0

    lax.fori_loop(0, n_chunks, score_body, 0)

    def count_ge(cand_col):
        counts = []
        for r0 in range(0, tq, SEQ_BLOCK):
            rows = slice(r0, r0 + SEQ_BLOCK)
            cand = jnp.broadcast_to(cand_col[rows], (SEQ_BLOCK, LANES))

            def body(j, acc, rows=rows, cand=cand):
                start = pl.multiple_of(j * tk, tk)
                for b in range(lane_blocks):
                    blk = key_ref[rows, pl.ds(start + b * LANES, LANES)]
                    acc = acc + jnp.where(blk >= cand, 1.0, 0.0)
                return acc

            counts.append(lax.fori_loop(0, n_chunks, body, jnp.zeros((SEQ_BLOCK, LANES), F32)))
        return jnp.sum(jnp.concatenate(counts, axis=0), axis=1, keepdims=True)

    kf = float(k_sel)
    thr = jnp.where(count_ge(jnp.zeros((tq, 1), I32)) >= kf, 0, INT_MIN).astype(I32)

    def bit_body(t, thr):
        cand = thr + jnp.left_shift(jnp.int32(1), 30 - t)
        return jnp.where(count_ge(cand) >= kf, cand, thr)

    thr = lax.fori_loop(0, 31, bit_body, thr)
    ties_kept = kf - count_ge(thr + 1)

    qs_ref[...] = _stack_heads(bq_ref[...])
    m_ref[...] = jnp.full(m_ref.shape, RUNNING_MAX_FLOOR, F32)
    acc_ref[...] = jnp.zeros(acc_ref.shape, F32)

    def attn_body(j, ties_before):
        start = pl.multiple_of(j * tk, tk)
        key = key_ref[:, pl.ds(start, tk)]
        tie = key == thr
        tie_count = ties_before + _dot(jnp.where(tie, 1.0, 0.0).astype(BF16), tri_ref[...])
        sel = ((key > thr) | (tie & (tie_count <= ties_kept))) & ((kcol + j * tk) <= qpos_col)
        bias_ref[...] = jnp.where(sel, 0.0, MASKED_SCORE)
        s_ref[...] = _dot_nt(qs_ref[...], bk_ref[pl.ds(start, tk), :])
        for h in range(B_HEADS):
            for r0 in range(0, tq, SEQ_BLOCK):
                rows = slice(h * tq + r0, h * tq + r0 + SEQ_BLOCK)
                s = s_ref[rows, :] + bias_ref[r0:r0 + SEQ_BLOCK, :]
                m_prev = m_ref[rows, :]
                m_new = jnp.maximum(m_prev, jnp.max(s, axis=1, keepdims=True))
                alpha_ref[rows, :] = jnp.exp2(m_prev - m_new)
                p_ref[rows, :] = jnp.exp2(s - jnp.concatenate([m_new] * lane_blocks, axis=1)).astype(BF16)
                m_ref[rows, :] = m_new
        acc_ref[...] = alpha_ref[...] * acc_ref[...] + _dot(p_ref[...], bv_ref[pl.ds(start, tk), :])
        return tie_count[:, tk - 1:tk]

    lax.fori_loop(0, n_chunks, attn_body, jnp.zeros((tq, 1), F32))
    lane = _lane_iota((tq, LANES))
    for c in range(B_HEADS // 2):
        even = acc_ref[(2 * c) * tq:(2 * c + 1) * tq, :]
        odd = acc_ref[(2 * c + 1) * tq:(2 * c + 2) * tq, :]
        o_ref[:, c * LANES:(c + 1) * LANES] = jnp.where(
            lane < HEAD_DIM, even / pltpu.roll(even, HEAD_DIM, 1), pltpu.roll(odd, HEAD_DIM, 1) / odd).astype(BF16)


def _dsa_attention(iq, iw, ik, bq, bk, bv, bsz, t_pad, k_sel):
    tq = _pick_tile(t_pad, (384, 256, 128))
    tk = tq
    nq = t_pad // tq
    qmap = lambda b, i: (b * nq + i, 0)
    kvmap = lambda b, i: (b, 0)
    tri = (jnp.arange(tk)[:, None] <= jnp.arange(tk)[None, :]).astype(BF16)
    stacked = pltpu.VMEM((B_HEADS * tq, LANES), BF16)
    stats = pltpu.VMEM((B_HEADS * tq, LANES), F32)
    return pl.pallas_call(
        functools.partial(_dsa_kernel, tq=tq, tk=tk, k_sel=k_sel),
        grid=(bsz, nq),
        in_specs=[pl.BlockSpec((tq, 512), qmap), pl.BlockSpec((tq, LANES), qmap), pl.BlockSpec((t_pad, LANES), kvmap),
                  pl.BlockSpec((tq, 512), qmap), pl.BlockSpec((t_pad, LANES), kvmap), pl.BlockSpec((t_pad, LANES), kvmap),
                  pl.BlockSpec((tk, tk), lambda b, i: (0, 0))],
        out_specs=pl.BlockSpec((tq, 512), qmap),
        out_shape=jax.ShapeDtypeStruct(bq.shape, BF16),
        scratch_shapes=[pltpu.VMEM((tq, t_pad), I32), stacked, stacked,
                        pltpu.VMEM((B_HEADS * tq, tk), F32), pltpu.VMEM((B_HEADS * tq, tk), BF16),
                        pltpu.VMEM((tq, tk), F32), stats, stats, stats],
        compiler_params=_params(("parallel", "arbitrary")),
        name="dsa_attn",
    )(iq, iw, ik, bq, bk, bv, tri)


def _merge_kernel(h_ref, a_ref, b_ref, ga_ref, gb_ref, wa_ref, wb_ref, wo_ref, gain_ref, wq_ref,
                  h1_o, hn_o, pq_o):
    ya = _dot(a_ref[...], wa_ref[...])
    yb = _dot(b_ref[...], wb_ref[...])
    mixed = ga_ref[...].astype(F32) * ya + gb_ref[...].astype(F32) * yb
    h1 = h_ref[...] + _dot(mixed.astype(BF16), wo_ref[...])
    h1_o[...] = h1
    hn = (h1 * lax.rsqrt(jnp.mean(h1 * h1, axis=-1, keepdims=True) + NORM_EPS) * gain_ref[...]).astype(BF16)
    hn_o[...] = hn
    pq_o[...] = _dot(hn, wq_ref[...]).astype(BF16)


def _merge(h2d, a_out, b_out, ga, gb, w_a, w_b, w_o, ffn_gain, w_query):
    n_tok, d_model = h2d.shape
    tm = _pick_tile(n_tok, (256, 128))
    row = lambda i: (i, 0)
    fixed = lambda i: (0, 0)
    pw = w_query.shape[1]
    return pl.pallas_call(
        _merge_kernel,
        grid=(n_tok // tm,),
        in_specs=[pl.BlockSpec((tm, d_model), row), pl.BlockSpec((tm, 512), row), pl.BlockSpec((tm, 512), row),
                  pl.BlockSpec((tm, d_model), row), pl.BlockSpec((tm, d_model), row),
                  pl.BlockSpec((512, d_model), fixed), pl.BlockSpec((512, d_model), fixed),
                  pl.BlockSpec((d_model, d_model), fixed), pl.BlockSpec((1, d_model), fixed),
                  pl.BlockSpec((d_model, pw), fixed)],
        out_specs=(pl.BlockSpec((tm, d_model), row), pl.BlockSpec((tm, d_model), row), pl.BlockSpec((tm, pw), row)),
        out_shape=(jax.ShapeDtypeStruct((n_tok, d_model), F32), jax.ShapeDtypeStruct((n_tok, d_model), BF16),
                   jax.ShapeDtypeStruct((n_tok, pw), BF16)),
        compiler_params=_params(("parallel",)),
        name="merge",
    )(h2d, a_out, b_out, ga, gb, w_a.astype(BF16), w_b.astype(BF16), w_o.astype(BF16),
      ffn_gain.reshape(1, d_model), w_query.astype(BF16))


def _top_values(x, count):
    vals = []
    cur = x
    rank = jnp.full(x.shape, float(count), F32)
    for it in range(count):
        m = jnp.max(cur, axis=0, keepdims=True)
        vals.append(m)
        hit = cur == m
        rank = jnp.where(hit, float(it), rank)
        cur = jnp.where(hit, -jnp.inf, cur)
    return jnp.concatenate(vals, axis=0), rank


def _pair_sum_candidates(v1, v2):
    k = v1.shape[0]
    pieces = []
    for a in range(k // 2):
        nb = min(k, -(-(k // (a + 1)) // 8) * 8)
        pieces.append(v1[a:a + 1] + v2[:nb])
    pieces.append(v1[k // 2:] + v2[0:1])
    return jnp.concatenate(pieces, axis=0)


def _peer_routing(pq_ref, sk_ref, cut_ref, e1_ref, rank_ref, e2_ref, n_keys):
    def head_body(h, _):
        lo = pl.multiple_of(h * LANES, LANES)
        st = _dot_nt(sk_ref[h], pq_ref[:, pl.ds(lo, LANES)])
        for c in range(st.shape[1] // LANES):
            cols = slice(c * LANES, (c + 1) * LANES)
            s1 = st[:n_keys, cols]
            s2 = st[n_keys:, cols]
            v1, _ = _top_values(s1, PEER_TOPK)
            v2, rank = _top_values(s2, PEER_TOPK)
            cand = _pair_sum_candidates(v1, v2)
            thr = _top_values(cand, PEER_TOPK)[0][PEER_TOPK - 1:PEER_TOPK]
            top = v1[0:1] + v2[0:1]
            z = jnp.sum(jnp.where(cand >= thr, jnp.exp(cand - top), 0.0), axis=0, keepdims=True)
            cut = jnp.zeros_like(s1)
            for b in range(PEER_TOPK):
                cut = cut + jnp.where(s1 + v2[b:b + 1] >= thr, 1.0, 0.0)
            cut_ref[h, :, cols] = cut
            e1_ref[h, :, cols] = jnp.exp(s1 - v1[0:1])
            rank_ref[h, :, cols] = rank.astype(BF16)
            e2_ref[h, :, cols] = (jnp.exp(s2 - v2[0:1]) / z).astype(BF16)
        return 0

    lax.fori_loop(0, PEER_HEADS, head_body, 0)


def _gelu_exact(x):
    return 0.5 * x * (1.0 + lax.erf(x * (1.0 / math.sqrt(2.0))))


def _peer_kernel(h1_ref, hn_ref, pq_ref, sk_ref, u_ref, vt_ref, o_ref,
                 cut_ref, e1_ref, rank_ref, e2_ref, act_ref, p_ref, acc_ref, *, te, grp, n_keys):
    s = pl.program_id(1)
    tm = hn_ref.shape[0]
    rows_per_tile = te // n_keys
    rows_per_grp = grp // n_keys
    n_groups = te // grp

    def weighted(tile, g):
        for c in range(tm // LANES):
            cols = slice(c * LANES, (c + 1) * LANES)
            tiles = {}
            for r in range(rows_per_grp):
                row = g * rows_per_grp + r
                k = row % 8
                w = jnp.zeros((n_keys, LANES), BF16)
                for h in range(PEER_HEADS):
                    if (h, row // 8) not in tiles:
                        row8 = pl.ds(pl.multiple_of(tile * rows_per_tile + (row // 8) * 8, 8), 8)
                        tiles[(h, row // 8)] = (cut_ref[h, row8, cols], e1_ref[h, row8, cols])
                    cut8, e18 = tiles[(h, row // 8)]
                    cut_b = jnp.broadcast_to(cut8[k:k + 1].astype(BF16), (n_keys, LANES))
                    e1_b = jnp.broadcast_to(e18[k:k + 1].astype(BF16), (n_keys, LANES))
                    hit = rank_ref[h, :, cols] < cut_b
                    w = w + jnp.where(hit, e2_ref[h, :, cols], jnp.zeros((), BF16)) * e1_b
                a = act_ref[row * n_keys:(row + 1) * n_keys, cols]
                p_ref[g, r * n_keys:(r + 1) * n_keys, cols] = w * _gelu_exact(a).astype(BF16)

    def accumulate(g):
        acc_ref[...] += _dot(vt_ref[:, g * grp:(g + 1) * grp], p_ref[g])

    @pl.when(s == 0)
    def _():
        _peer_routing(pq_ref, sk_ref, cut_ref, e1_ref, rank_ref, e2_ref, n_keys)
        acc_ref[...] = jnp.zeros_like(acc_ref)

    act_ref[...] = _dot_nt(u_ref[...], hn_ref[...])
    for g in range(n_groups):
        weighted(s, g)
        if g > 0:
            accumulate(g - 1)
    accumulate(n_groups - 1)

    @pl.when(s == pl.num_programs(1) - 1)
    def _():
        o_ref[...] = h1_ref[...] + acc_ref[...].T


def _peer(h1, hn, pq, sub_keys, peer_u, peer_v):
    n_tok, d_model = h1.shape
    n_heads, _, n_keys, half = sub_keys.shape
    n_experts = peer_u.shape[0]
    assert n_heads == PEER_HEADS and n_keys == LANES and 2 * half == LANES and n_experts == n_keys * n_keys
    tm = _pick_tile(n_tok, (512, 384, 256, 128))
    te = 2048
    grp = 256
    zeros = jnp.zeros((n_heads, n_keys, half), sub_keys.dtype)
    sk = jnp.concatenate([jnp.concatenate([sub_keys[:, 0], zeros], axis=2),
                          jnp.concatenate([zeros, sub_keys[:, 1]], axis=2)], axis=1).astype(BF16)
    u = peer_u.astype(BF16)
    vt = peer_v.astype(BF16).T
    row = lambda i, j: (i, 0)
    route_f32 = pltpu.VMEM((PEER_HEADS, n_keys, tm), F32)
    route_bf16 = pltpu.VMEM((PEER_HEADS, n_keys, tm), BF16)
    return pl.pallas_call(
        functools.partial(_peer_kernel, te=te, grp=grp, n_keys=n_keys),
        grid=(n_tok // tm, n_experts // te),
        in_specs=[pl.BlockSpec((tm, d_model), row), pl.BlockSpec((tm, d_model), row),
                  pl.BlockSpec((tm, PEER_HEADS * LANES), row),
                  pl.BlockSpec((n_heads, 2 * n_keys, LANES), lambda i, s: (0, 0, 0)),
                  pl.BlockSpec((te, d_model), lambda i, s: (s, 0)),
                  pl.BlockSpec((d_model, te), lambda i, s: (0, s))],
        out_specs=pl.BlockSpec((tm, d_model), row),
        out_shape=jax.ShapeDtypeStruct((n_tok, d_model), F32),
        scratch_shapes=[route_f32, route_f32, route_bf16, route_bf16, pltpu.VMEM((te, tm), F32),
                        pltpu.VMEM((te // grp, grp, tm), BF16), pltpu.VMEM((d_model, tm), F32)],
        compiler_params=_params(("parallel", "arbitrary")),
        name="peer",
    )(h1, hn, pq, sk, u, vt)


def kernel(x, meta_tokens, mix_norm_gain, w_in, a_q_norm_gain, a_k_norm_gain, a_lambda_q1, a_lambda_k1, a_lambda_q2, a_lambda_k2, a_subln_gain, w_branch_a, b_q_norm_gain, b_k_norm_gain, w_branch_b, w_out, ffn_norm_gain, peer_w_query, peer_sub_keys, peer_u, peer_v):
    bsz, seq, d_model = x.shape
    depth = w_in.shape[0]
    t_real = N_META_TOKENS + seq
    t_pad = ((t_real + SEQ_BLOCK - 1) // SEQ_BLOCK) * SEQ_BLOCK
    k_sel = min(TOPK_LIMIT, seq // 4)

    meta = jnp.broadcast_to(meta_tokens.astype(x.dtype)[None], (bsz, N_META_TOKENS, d_model))
    pad = jnp.zeros((bsz, t_pad - t_real, d_model), x.dtype)
    h = jnp.concatenate([meta, x, pad], axis=1).reshape(bsz * t_pad, d_model)

    q_scale = HEAD_DIM ** -0.5 * LOG2_E
    for layer in range(depth):
        lambda_init = 0.8 - 0.6 * math.exp(-0.3 * layer)
        aqg = (jnp.tile(a_q_norm_gain[layer], 8) * q_scale).reshape(1, 512)
        akg = jnp.tile(a_k_norm_gain[layer], 8).reshape(1, 512)
        bqg = (jnp.tile(b_q_norm_gain[layer], 8) * q_scale).reshape(1, 512)
        bkg = jnp.tile(b_k_norm_gain[layer], 2).reshape(1, LANES)
        p = _input_projection(h, t_pad, mix_norm_gain[layer], w_in[layer], aqg, akg, bqg, bkg)
        vec = lambda a: a[layer].astype(F32).reshape(1, HEAD_DIM)
        a_out = _diff_attention(p["aq"], p["ak"], p["av"], bsz, t_pad, vec(a_lambda_q1), vec(a_lambda_k1),
                                vec(a_lambda_q2), vec(a_lambda_k2), a_subln_gain[layer].reshape(1, LANES), lambda_init)
        b_out = _dsa_attention(p["iq"], p["iw"], p["ik"], p["bq"], p["bk"], p["bv"], bsz, t_pad, k_sel)
        h1, hn, pq = _merge(h, a_out, b_out, p["ga"], p["gb"], w_branch_a[layer], w_branch_b[layer], w_out[layer],
                            ffn_norm_gain[layer], peer_w_query[layer])
        h = _peer(h1, hn, pq, peer_sub_keys[layer], peer_u[layer], peer_v[layer])

    return h.reshape(bsz, t_pad, d_model)[:, N_META_TOKENS:N_META_TOKENS + seq]
```

```python
import functools
import math

import jax
import jax.numpy as jnp
from jax import lax
from jax.experimental import pallas as pl
from jax.experimental.pallas import tpu as pltpu

F32 = jnp.float32
BF16 = jnp.bfloat16
I32 = jnp.int32

N_META_TOKENS = 16
SEQ_BLOCK = 128
ROPE_BASE = 10000.0
NORM_EPS = 1e-6
HEAD_DIM = 64
LANES = 128
A_HEADS = 4
B_HEADS = 8
IDX_HEADS = 8
IDX_ROPE = 32
TOPK_LIMIT = 256
PEER_HEADS = 8
PEER_TOPK = 16
MASKED_SCORE = -1e30
RUNNING_MAX_FLOOR = -1e29
INT_MIN = -(2 ** 31)
LOG2_E = math.log2(math.e)
VMEM_LIMIT = 56 * 1024 * 1024


def _pick_tile(n, candidates):
    for c in candidates:
        if n % c == 0:
            return c
    raise ValueError(f"no tile in {candidates} divides {n}")


def _params(semantics):
    return pltpu.CompilerParams(dimension_semantics=semantics, vmem_limit_bytes=VMEM_LIMIT)


def _dot(a, b):
    return jnp.dot(a, b, preferred_element_type=F32)


def _dot_nt(a, b):
    return lax.dot_general(a, b, (((1,), (1,)), ((), ())), preferred_element_type=F32)


def _lane_iota(shape):
    return lax.broadcasted_iota(I32, shape, 1)


_SEC = {}
_off = 0
for _name, _w in (("aq", 512), ("ak", 512), ("av", 512), ("bq", 512), ("bk", 128), ("bv", 128),
                  ("iq", 512), ("ik", 128), ("iw", 128), ("ga", 1024), ("gb", 1024)):
    _SEC[_name] = (_off, _off + _w)
    _off += _w
IN_COLS = _off


def _head_norm(x, gain, ones_blk):
    w = x.shape[1]
    x2 = x * x
    hi = x2.astype(BF16)
    lo = (x2 - hi.astype(F32)).astype(BF16)
    m = ones_blk[:w, :w]
    ms = _dot(hi, m) + _dot(lo, m)
    return x * lax.rsqrt(ms + NORM_EPS) * gain


def _rotary(x, cos, sin_signed, half):
    w = x.shape[1]
    lane = _lane_iota(x.shape)
    fwd = pltpu.roll(x, w - half, 1)
    bwd = pltpu.roll(x, half, 1)
    partner = jnp.where((lane & half) == 0, fwd, bwd)
    return x * cos + partner * sin_signed


def _inproj_kernel(h_ref, gain_ref, w_ref, ones_ref, cos_ref, sin_ref, icos_ref, isin_ref,
                   aqg_ref, akg_ref, bqg_ref, bkg_ref,
                   aq_o, ak_o, av_o, bq_o, bk_o, bv_o, iq_o, ik_o, iw_o, ga_o, gb_o, *, iw_scale):
    h = h_ref[...]
    xn = (h * lax.rsqrt(jnp.mean(h * h, axis=-1, keepdims=True) + NORM_EPS) * gain_ref[...]).astype(BF16)

    def proj(name):
        a, b = _SEC[name]
        return _dot(xn, w_ref[:, a:b])

    ones_blk = ones_ref[...]
    cos = cos_ref[...]
    sin = sin_ref[...]
    aq_o[...] = _rotary(_head_norm(proj("aq"), aqg_ref[...], ones_blk), cos, sin, 32).astype(BF16)
    ak_o[...] = _rotary(_head_norm(proj("ak"), akg_ref[...], ones_blk), cos, sin, 32).astype(BF16)
    av_o[...] = proj("av").astype(BF16)
    bq_o[...] = _rotary(_head_norm(proj("bq"), bqg_ref[...], ones_blk), cos, sin, 32).astype(BF16)
    bk_o[...] = _rotary(_head_norm(proj("bk"), bkg_ref[...], ones_blk), cos[:, :LANES], sin[:, :LANES], 32).astype(BF16)
    bv = proj("bv")
    bv_o[...] = jnp.where(_lane_iota(bv.shape) < HEAD_DIM, bv, 1.0).astype(BF16)
    icos = icos_ref[...]
    isin = isin_ref[...]
    iq_o[...] = _rotary(proj("iq"), icos, isin, 16).astype(BF16)
    ik_o[...] = _rotary(proj("ik"), icos[:, :LANES], isin[:, :LANES], 16).astype(BF16)
    iw_o[...] = proj("iw") * iw_scale
    ga_o[...] = jax.nn.sigmoid(proj("ga")).astype(BF16)
    gb_o[...] = jax.nn.sigmoid(proj("gb")).astype(BF16)


def _rope_tables(t_len, dim, width_dim):
    inv = ROPE_BASE ** (-jnp.arange(0, dim, 2, dtype=F32) / dim)
    ang = jnp.arange(t_len, dtype=F32)[:, None] * inv[None, :]
    cos, sin = jnp.cos(ang), jnp.sin(ang)
    rest = width_dim - dim
    cos_h = jnp.concatenate([cos, cos, jnp.ones((t_len, rest), F32)], axis=1)
    sin_h = jnp.concatenate([-sin, sin, jnp.zeros((t_len, rest), F32)], axis=1)
    return cos_h, sin_h


def _input_projection(h2d, t_pad, mix_gain, w_in, aqg, akg, bqg, bkg):
    n_tok, d_model = h2d.shape
    tm = _pick_tile(t_pad, (384, 256, 128))
    per_seq = t_pad // tm

    sizes = (512, 512, 512, 512, 64, 64, 512, 64, IDX_HEADS, d_model, d_model)
    offs = [0]
    for s in sizes:
        offs.append(offs[-1] + s)
    cols = [w_in[:, offs[i]:offs[i + 1]] for i in range(len(sizes))]
    waq, wak, wav, wbq, wbk, wbv, wiq, wik, wiw, wga, wgb = cols
    wiw = jnp.pad(wiw, ((0, 0), (0, LANES - IDX_HEADS)))
    w_all = jnp.concatenate([waq, wak, wav, wbq, wbk, wbk, wbv, wbv, wiq, wik, wik, wiw, wga, wgb],
                            axis=1).astype(BF16)
    assert w_all.shape[1] == IN_COLS

    cos_h, sin_h = _rope_tables(t_pad, HEAD_DIM, HEAD_DIM)
    cos = jnp.tile(cos_h, (1, 8))
    sin = jnp.tile(sin_h, (1, 8))
    icos_h, isin_h = _rope_tables(t_pad, IDX_ROPE, HEAD_DIM)
    icos = jnp.tile(icos_h, (1, 8))
    isin = jnp.tile(isin_h, (1, 8))
    head_id = jnp.arange(512) // HEAD_DIM
    ones_blk = jnp.where(head_id[:, None] == head_id[None, :], 1.0 / HEAD_DIM, 0.0).astype(BF16)

    row = lambda i: (i, 0)
    fixed = lambda i: (0, 0)
    tab = lambda i: (i % per_seq, 0)
    widths = dict(aq=512, ak=512, av=512, bq=512, bk=128, bv=128, iq=512, ik=128, iw=128, ga=d_model, gb=d_model)
    out_dtypes = dict(iw=F32)
    names = ("aq", "ak", "av", "bq", "bk", "bv", "iq", "ik", "iw", "ga", "gb")
    out_shape = tuple(jax.ShapeDtypeStruct((n_tok, widths[k]), out_dtypes.get(k, BF16)) for k in names)
    out_specs = tuple(pl.BlockSpec((tm, widths[k]), row) for k in names)
    in_specs = [
        pl.BlockSpec((tm, d_model), row),
        pl.BlockSpec((1, d_model), fixed),
        pl.BlockSpec((d_model, IN_COLS), fixed),
        pl.BlockSpec((512, 512), fixed),
        pl.BlockSpec((tm, 512), tab), pl.BlockSpec((tm, 512), tab),
        pl.BlockSpec((tm, 512), tab), pl.BlockSpec((tm, 512), tab),
        pl.BlockSpec((1, 512), fixed), pl.BlockSpec((1, 512), fixed),
        pl.BlockSpec((1, 512), fixed), pl.BlockSpec((1, LANES), fixed),
    ]
    iw_scale = IDX_HEADS ** -0.5 * HEAD_DIM ** -0.5
    outs = pl.pallas_call(
        functools.partial(_inproj_kernel, iw_scale=iw_scale),
        grid=(n_tok // tm,),
        in_specs=in_specs, out_specs=out_specs, out_shape=out_shape,
        compiler_params=_params(("parallel",)),
        name="inproj",
    )(h2d, mix_gain.reshape(1, d_model), w_all, ones_blk, cos, sin, icos, isin, aqg, akg, bqg, bkg)
    return dict(zip(names, outs))


def _stack_heads(x):
    rows, width = x.shape
    lane = _lane_iota((rows, LANES))
    zero = jnp.zeros((rows, LANES), x.dtype)
    groups = []
    for c in range(width // LANES):
        blk = x[:, c * LANES:(c + 1) * LANES]
        groups.append(jnp.where(lane < HEAD_DIM, blk, zero))
        groups.append(jnp.where(lane >= HEAD_DIM, blk, zero))
    return jnp.concatenate(groups, axis=0)


def _diff_kernel(q_ref, k_ref, v_ref, lq1_ref, lk1_ref, lq2_ref, lk2_ref, gain_ref, o_ref,
                 qs_ref, s_ref, p_ref, m_ref, l_ref, alpha_ref, acc_ref, *, tq, tk, lambda_init):
    i = pl.program_id(2)
    rows = 2 * tq
    lane_blocks = tk // LANES
    qs_ref[...] = _stack_heads(q_ref[...])
    m_ref[...] = jnp.full(m_ref.shape, RUNNING_MAX_FLOOR, F32)
    l_ref[...] = jnp.zeros(l_ref.shape, F32)
    acc_ref[...] = jnp.zeros(acc_ref.shape, F32)

    def body(j, _):
        start = pl.multiple_of(j * tk, tk)
        s_ref[...] = _dot_nt(qs_ref[...], k_ref[pl.ds(start, tk), :])
        for r0 in range(0, rows, SEQ_BLOCK):
            blk = slice(r0, r0 + SEQ_BLOCK)
            qpos = i * tq + r0 % tq + lax.broadcasted_iota(I32, (SEQ_BLOCK, 1), 0)
            s = jnp.where((_lane_iota((SEQ_BLOCK, tk)) + j * tk) <= qpos, s_ref[blk, :], MASKED_SCORE)
            m_prev = m_ref[blk, :]
            m_new = jnp.maximum(m_prev, jnp.max(s, axis=1, keepdims=True))
            alpha = jnp.exp2(m_prev - m_new)
            p = jnp.exp2(s - jnp.concatenate([m_new] * lane_blocks, axis=1))
            l_ref[blk, :] = alpha * l_ref[blk, :] + jnp.sum(p, axis=1, keepdims=True)
            alpha_ref[blk, :] = alpha
            p_ref[blk, :] = p.astype(BF16)
            m_ref[blk, :] = m_new
        acc_ref[...] = alpha_ref[...] * acc_ref[...] + _dot(p_ref[...], v_ref[pl.ds(start, tk), :])
        return 0

    n_chunks = ((i + 1) * tq + tk - 1) // tk
    lax.fori_loop(0, n_chunks, body, 0)
    o = acc_ref[...] / l_ref[...]
    lam =(jnp.exp(jnp.sum(lq1_ref[...] * lk1_ref[...], axis=-1, keepdims=True))
           - jnp.exp(jnp.sum(lq2_ref[...] * lk2_ref[...], axis=-1, keepdims=True)) + lambda_init)
    d = o[:tq] - lam * o[tq:]
    d = d * lax.rsqrt(jnp.mean(d * d, axis=-1, keepdims=True) + NORM_EPS) * gain_ref[...]
    o_ref[...] = (d * (1.0 - lambda_init)).astype(BF16)


def _diff_attention(aq, ak, av, bsz, t_pad, lq1, lk1, lq2, lk2, subln_gain, lambda_init):
    tq = _pick_tile(t_pad, (384, 256, 128))
    tk = tq
    nq = t_pad // tq
    qmap = lambda b, h, i: (b * nq + i, h)
    kvmap = lambda b, h, i: (b, h)
    fixed = lambda b, h, i: (0, 0)
    vec = pl.BlockSpec((1, HEAD_DIM), fixed)
    stats = pltpu.VMEM((2 * tq, LANES), F32)
    return pl.pallas_call(
        functools.partial(_diff_kernel, tq=tq, tk=tk, lambda_init=lambda_init),
        grid=(bsz, A_HEADS, nq),
        in_specs=[pl.BlockSpec((tq, LANES), qmap), pl.BlockSpec((t_pad, LANES), kvmap),
                  pl.BlockSpec((t_pad, LANES), kvmap), vec, vec, vec, vec, pl.BlockSpec((1, LANES), fixed)],
        out_specs=pl.BlockSpec((tq, LANES), qmap),
        out_shape=jax.ShapeDtypeStruct(aq.shape, BF16),
        scratch_shapes=[pltpu.VMEM((2 * tq, LANES), BF16), pltpu.VMEM((2 * tq, tk), F32),
                        pltpu.VMEM((2 * tq, tk), BF16), stats, stats, stats, stats],
        compiler_params=_params(("parallel", "parallel", "arbitrary")),
        name="diff_attn",
    )(aq, ak, av, lq1, lk1, lq2, lk2, subln_gain)


def _dsa_kernel(iq_ref, iw_ref, ik_ref, bq_ref, bk_ref, bv_ref, tri_ref, o_ref,
                key_ref, iqs_ref, qs_ref, s_ref, p_ref, bias_ref, m_ref, alpha_ref, acc_ref, *, tq, tk, k_sel):
    i = pl.program_id(1)
    n_chunks = ((i + 1) * tq + tk - 1) // tk
    lane_blocks = tk // LANES
    qpos_col = i * tq + lax.broadcasted_iota(I32, (tq, 1), 0)
    kcol = _lane_iota((tq, tk))

    iqs_ref[...] = _stack_heads(iq_ref[...])
    iw = iw_ref[...]

    def score_body(j, _):
        start = pl.multiple_of(j * tk, tk)
        s_ref[...] = _dot_nt(iqs_ref[...], ik_ref[pl.ds(start, tk), :])
        for r0 in range(0, tq, SEQ_BLOCK):
            rows = slice(r0, r0 + SEQ_BLOCK)
            score = jnp.zeros((SEQ_BLOCK, tk), F32)
            for h in range(IDX_HEADS):
                logits = s_ref[h * tq + r0:h * tq + r0 + SEQ_BLOCK, :]
                score = score + jnp.maximum(logits, 0.0) * iw_ref[rows, h:h + 1]
            qpos_blk = i * tq + r0 + lax.broadcasted_iota(I32, (SEQ_BLOCK, 1), 0)
            score = jnp.where((_lane_iota((SEQ_BLOCK, tk)) + j * tk) <= qpos_blk, score, -jnp.inf)
            bits = lax.bitcast_convert_type(score, I32)
            key_ref[rows, pl.ds(start, tk)] = jnp.where(bits < 0, (bits ^ 0x7FFFFFFF) + 1, bits)
        return 0

    lax.fori_loop(0, n_chunks, score_body, 0)

    def count_ge(cand_col):
        counts = []
        for r0 in range(0, tq, SEQ_BLOCK):
            rows = slice(r0, r0 + SEQ_BLOCK)
            cand = jnp.broadcast_to(cand_col[rows], (SEQ_BLOCK, LANES))

            def body(j, acc, rows=rows, cand=cand):
                start = pl.multiple_of(j * tk, tk)
                for b in range(lane_blocks):
                    blk = key_ref[rows, pl.ds(start + b * LANES, LANES)]
                    acc = acc + jnp.where(blk >= cand, 1.0, 0.0)
                return acc

            counts.append(lax.fori_loop(0, n_chunks, body, jnp.zeros((SEQ_BLOCK, LANES), F32)))
        return jnp.sum(jnp.concatenate(counts, axis=0), axis=1, keepdims=True)

    kf = float(k_sel)
    thr = jnp.where(count_ge(jnp.zeros((tq, 1), I32)) >= kf, 0, INT_MIN).astype(I32)

    def bit_body(t, thr):
        cand = thr + jnp.left_shift(jnp.int32(1), 30 - t)
        return jnp.where(count_ge(cand) >= kf, cand, thr)

    thr = lax.fori_loop(0, 31, bit_body, thr)
    ties_kept = kf - count_ge(thr + 1)

    qs_ref[...] = _stack_heads(bq_ref[...])
    m_ref[...] = jnp.full(m_ref.shape, RUNNING_MAX_FLOOR, F32)
    acc_ref[...] = jnp.zeros(acc_ref.shape, F32)

    def attn_body(j, ties_before):
        start = pl.multiple_of(j * tk, tk)
        key = key_ref[:, pl.ds(start, tk)]
        tie = key == thr
        tie_count = ties_before + _dot(jnp.where(tie, 1.0, 0.0).astype(BF16), tri_ref[...])
        sel = ((key > thr) | (tie & (tie_count <= ties_kept))) & ((kcol + j * tk) <= qpos_col)
        bias_ref[...] = jnp.where(sel, 0.0, MASKED_SCORE)
        s_ref[...] = _dot_nt(qs_ref[...], bk_ref[pl.ds(start, tk), :])
        for h in range(B_HEADS):
            for r0 in range(0, tq, SEQ_BLOCK):
                rows = slice(h * tq + r0, h * tq + r0 + SEQ_BLOCK)
                s = s_ref[rows, :] + bias_ref[r0:r0 + SEQ_BLOCK, :]
                m_prev = m_ref[rows, :]
                m_new = jnp.maximum(m_prev, jnp.max(s, axis=1, keepdims=True))
                alpha_ref[rows, :] = jnp.exp2(m_prev - m_new)
                p_ref[rows, :] = jnp.exp2(s - jnp.concatenate([m_new] * lane_blocks, axis=1)).astype(BF16)
                m_ref[rows, :] = m_new
        acc_ref[...] = alpha_ref[...] * acc_ref[...] + _dot(p_ref[...], bv_ref[pl.ds(start, tk), :])
        return tie_count[:, tk - 1:tk]

    lax.fori_loop(0, n_chunks, attn_body, jnp.zeros((tq, 1), F32))
    lane = _lane_iota((tq, LANES))
    for c in range(B_HEADS // 2):
        even = acc_ref[(2 * c) * tq:(2 * c + 1) * tq, :]
        odd = acc_ref[(2 * c + 1) * tq:(2 * c + 2) * tq, :]
        o_ref[:, c * LANES:(c + 1) * LANES] = jnp.where(
            lane < HEAD_DIM, even / pltpu.roll(even, HEAD_DIM, 1), pltpu.roll(odd, HEAD_DIM, 1) / odd).astype(BF16)


def _dsa_attention(iq, iw, ik, bq, bk, bv, bsz, t_pad, k_sel):
    tq = _pick_tile(t_pad, (384, 256, 128))
    tk = tq
    nq = t_pad // tq
    qmap = lambda b, i: (b * nq + i, 0)
    kvmap = lambda b, i: (b, 0)
    tri = (jnp.arange(tk)[:, None] <= jnp.arange(tk)[None, :]).astype(BF16)
    stacked = pltpu.VMEM((B_HEADS * tq, LANES), BF16)
    stats = pltpu.VMEM((B_HEADS * tq, LANES), F32)
    return pl.pallas_call(
        functools.partial(_dsa_kernel, tq=tq, tk=tk, k_sel=k_sel),
        grid=(bsz, nq),
        in_specs=[pl.BlockSpec((tq, 512), qmap), pl.BlockSpec((tq, LANES), qmap), pl.BlockSpec((t_pad, LANES), kvmap),
                  pl.BlockSpec((tq, 512), qmap), pl.BlockSpec((t_pad, LANES), kvmap), pl.BlockSpec((t_pad, LANES), kvmap),
                  pl.BlockSpec((tk, tk), lambda b, i: (0, 0))],
        out_specs=pl.BlockSpec((tq, 512), qmap),
        out_shape=jax.ShapeDtypeStruct(bq.shape, BF16),
        scratch_shapes=[pltpu.VMEM((tq, t_pad), I32), stacked, stacked,
                        pltpu.VMEM((B_HEADS * tq, tk), F32), pltpu.VMEM((B_HEADS * tq, tk), BF16),
                        pltpu.VMEM((tq, tk), F32), stats, stats, stats],
        compiler_params=_params(("parallel", "arbitrary")),
        name="dsa_attn",
    )(iq, iw, ik, bq, bk, bv, tri)


def _merge_kernel(h_ref, a_ref, b_ref, ga_ref, gb_ref, wa_ref, wb_ref, wo_ref, gain_ref, wq_ref,
                  h1_o, hn_o, pq_o):
    ya = _dot(a_ref[...], wa_ref[...])
    yb = _dot(b_ref[...], wb_ref[...])
    mixed = ga_ref[...].astype(F32) * ya + gb_ref[...].astype(F32) * yb
    h1 = h_ref[...] + _dot(mixed.astype(BF16), wo_ref[...])
    h1_o[...] = h1
    hn = (h1 * lax.rsqrt(jnp.mean(h1 * h1, axis=-1, keepdims=True) + NORM_EPS) * gain_ref[...]).astype(BF16)
    hn_o[...] = hn
    pq_o[...] = _dot(hn, wq_ref[...]).astype(BF16)


def _merge(h2d, a_out, b_out, ga, gb, w_a, w_b, w_o, ffn_gain, w_query):
    n_tok, d_model = h2d.shape
    tm = _pick_tile(n_tok, (256, 128))
    row = lambda i: (i, 0)
    fixed = lambda i: (0, 0)
    pw = w_query.shape[1]
    return pl.pallas_call(
        _merge_kernel,
        grid=(n_tok // tm,),
        in_specs=[pl.BlockSpec((tm, d_model), row), pl.BlockSpec((tm, 512), row), pl.BlockSpec((tm, 512), row),
                  pl.BlockSpec((tm, d_model), row), pl.BlockSpec((tm, d_model), row),
                  pl.BlockSpec((512, d_model), fixed), pl.BlockSpec((512, d_model), fixed),
                  pl.BlockSpec((d_model, d_model), fixed), pl.BlockSpec((1, d_model), fixed),
                  pl.BlockSpec((d_model, pw), fixed)],
        out_specs=(pl.BlockSpec((tm, d_model), row), pl.BlockSpec((tm, d_model), row), pl.BlockSpec((tm, pw), row)),
        out_shape=(jax.ShapeDtypeStruct((n_tok, d_model), F32), jax.ShapeDtypeStruct((n_tok, d_model), BF16),
                   jax.ShapeDtypeStruct((n_tok, pw), BF16)),
        compiler_params=_params(("parallel",)),
        name="merge",
    )(h2d, a_out, b_out, ga, gb, w_a.astype(BF16), w_b.astype(BF16), w_o.astype(BF16),
      ffn_gain.reshape(1, d_model), w_query.astype(BF16))


def _top_values(x, count):
    vals = []
    cur = x
    rank = jnp.full(x.shape, float(count), F32)
    for it in range(count):
        m = jnp.max(cur, axis=0, keepdims=True)
        vals.append(m)
        hit = cur == m
        rank = jnp.where(hit, float(it), rank)
        cur = jnp.where(hit, -jnp.inf, cur)
    return jnp.concatenate(vals, axis=0), rank


def _pair_sum_candidates(v1, v2):
    k = v1.shape[0]
    pieces = []
    for a in range(k // 2):
        nb = min(k, -(-(k // (a + 1)) // 8) * 8)
        pieces.append(v1[a:a + 1] + v2[:nb])
    pieces.append(v1[k // 2:] + v2[0:1])
    return jnp.concatenate(pieces, axis=0)


def _peer_routing(pq_ref, sk_ref, cut_ref, e1_ref, rank_ref, e2_ref, n_keys):
    def head_body(h, _):
        lo = pl.multiple_of(h * LANES, LANES)
        st = _dot_nt(sk_ref[h], pq_ref[:, pl.ds(lo, LANES)])
        for c in range(st.shape[1] // LANES):
            cols = slice(c * LANES, (c + 1) * LANES)
            s1 = st[:n_keys, cols]
            s2 = st[n_keys:, cols]
            v1, _ = _top_values(s1, PEER_TOPK)
            v2, rank = _top_values(s2, PEER_TOPK)
            cand = _pair_sum_candidates(v1, v2)
            thr = _top_values(cand, PEER_TOPK)[0][PEER_TOPK - 1:PEER_TOPK]
            top = v1[0:1] + v2[0:1]
            z = jnp.sum(jnp.where(cand >= thr, jnp.exp(cand - top), 0.0), axis=0, keepdims=True)
            cut = jnp.zeros_like(s1)
            for b in range(PEER_TOPK):
                cut = cut + jnp.where(s1 + v2[b:b + 1] >= thr, 1.0, 0.0)
            cut_ref[h, c] = cut
            e1_ref[h, c] = jnp.exp(s1 - v1[0:1])
            rank_ref[h, c] = rank
            e2_ref[h, c] = jnp.exp(s2 - v2[0:1]) / z
        return 0

    lax.fori_loop(0, PEER_HEADS, head_body, 0)


def _gelu_exact(x):
    return 0.5 * x * (1.0 + lax.erf(x * (1.0 / math.sqrt(2.0))))


def _peer_kernel(h1_ref, hn_ref, pq_ref, sk_ref, u_ref, vt_ref, o_ref,
                 cut_ref, e1_ref, rank_ref, e2_ref, act_ref, p_ref, acc_ref, *, te, n_keys):
    s = pl.program_id(1)
    tm = hn_ref.shape[0]
    rows_per_tile = te // n_keys

    @pl.when(s == 0)
    def _():
        _peer_routing(pq_ref, sk_ref, cut_ref, e1_ref, rank_ref, e2_ref, n_keys)
        acc_ref[...] = jnp.zeros_like(acc_ref)

    act_ref[...] = _dot_nt(u_ref[...], hn_ref[...])

    def chunk_body(c, _):
        cols = pl.ds(pl.multiple_of(c * LANES, LANES), LANES)
        for row8 in range(rows_per_tile // 8):
            rows = pl.ds(pl.multiple_of(s * rows_per_tile + row8 * 8, 8), 8)
            cut8 = [cut_ref[h, c, rows, :] for h in range(PEER_HEADS)]
            e18 = [e1_ref[h, c, rows, :] for h in range(PEER_HEADS)]
            for k in range(8):
                row = row8 * 8 + k
                w = jnp.zeros((n_keys, LANES), F32)
                for h in range(PEER_HEADS):
                    cut_b = jnp.broadcast_to(cut8[h][k:k + 1], (n_keys, LANES))
                    e1_b = jnp.broadcast_to(e18[h][k:k + 1], (n_keys, LANES))
                    w = w + jnp.where(rank_ref[h, c] < cut_b, e2_ref[h, c], 0.0) * e1_b
                a = act_ref[row * n_keys:(row + 1) * n_keys, cols]
                p_ref[row * n_keys:(row + 1) * n_keys, cols] = (w * _gelu_exact(a)).astype(BF16)
        return 0

    lax.fori_loop(0, tm // LANES, chunk_body, 0)
    acc_ref[...] += _dot(vt_ref[...], p_ref[...])

    @pl.when(s == pl.num_programs(1) - 1)
    def _():
        o_ref[...] = h1_ref[...] + acc_ref[...].T


def _peer(h1, hn, pq, sub_keys, peer_u, peer_v):
    n_tok, d_model = h1.shape
    n_heads, _, n_keys, half = sub_keys.shape
    n_experts = peer_u.shape[0]
    assert n_heads == PEER_HEADS and n_keys == LANES and 2 * half == LANES and n_experts == n_keys * n_keys
    tm = _pick_tile(n_tok, (512, 384, 256, 128))
    te = 2048
    zeros = jnp.zeros((n_heads, n_keys, half), sub_keys.dtype)
    sk = jnp.concatenate([jnp.concatenate([sub_keys[:, 0], zeros], axis=2),
                          jnp.concatenate([zeros, sub_keys[:, 1]], axis=2)], axis=1).astype(BF16)
    u = peer_u.astype(BF16)
    vt = peer_v.astype(BF16).T
    row = lambda i, j: (i, 0)
    route = pltpu.VMEM((PEER_HEADS, tm // LANES, n_keys, LANES), F32)
    return pl.pallas_call(
        functools.partial(_peer_kernel, te=te, n_keys=n_keys),
        grid=(n_tok // tm, n_experts // te),
        in_specs=[pl.BlockSpec((tm, d_model), row), pl.BlockSpec((tm, d_model), row),
                  pl.BlockSpec((tm, PEER_HEADS * LANES), row),
                  pl.BlockSpec((n_heads, 2 * n_keys, LANES), lambda i, s: (0, 0, 0)),
                  pl.BlockSpec((te, d_model), lambda i, s: (s, 0)),
                  pl.BlockSpec((d_model, te), lambda i, s: (0, s))],
        out_specs=pl.BlockSpec((tm, d_model), row),
        out_shape=jax.ShapeDtypeStruct((n_tok, d_model), F32),
        scratch_shapes=[route, route, route, route, pltpu.VMEM((te, tm), F32),
                        pltpu.VMEM((te, tm), BF16), pltpu.VMEM((d_model, tm), F32)],
        compiler_params=_params(("parallel", "arbitrary")),
        name="peer",
    )(h1, hn, pq, sk, u, vt)


def kernel(x, meta_tokens, mix_norm_gain, w_in, a_q_norm_gain, a_k_norm_gain, a_lambda_q1, a_lambda_k1, a_lambda_q2, a_lambda_k2, a_subln_gain, w_branch_a, b_q_norm_gain, b_k_norm_gain, w_branch_b, w_out, ffn_norm_gain, peer_w_query, peer_sub_keys, peer_u, peer_v):
    bsz, seq, d_model = x.shape
    depth = w_in.shape[0]
    t_real = N_META_TOKENS + seq
    t_pad = ((t_real + SEQ_BLOCK - 1) // SEQ_BLOCK) * SEQ_BLOCK
    k_sel = min(TOPK_LIMIT, seq // 4)

    meta = jnp.broadcast_to(meta_tokens.astype(x.dtype)[None], (bsz, N_META_TOKENS, d_model))
    pad = jnp.zeros((bsz, t_pad - t_real, d_model), x.dtype)
    h = jnp.concatenate([meta, x, pad], axis=1).reshape(bsz * t_pad, d_model)

    q_scale = HEAD_DIM ** -0.5 * LOG2_E
    for layer in range(depth):
        lambda_init = 0.8 - 0.6 * math.exp(-0.3 * layer)
        aqg = (jnp.tile(a_q_norm_gain[layer], 8) * q_scale).reshape(1, 512)
        akg = jnp.tile(a_k_norm_gain[layer], 8).reshape(1, 512)
        bqg = (jnp.tile(b_q_norm_gain[layer], 8) * q_scale).reshape(1, 512)
        bkg = jnp.tile(b_k_norm_gain[layer], 2).reshape(1, LANES)
        p = _input_projection(h, t_pad, mix_norm_gain[layer], w_in[layer], aqg, akg, bqg, bkg)
        vec = lambda a: a[layer].astype(F32).reshape(1, HEAD_DIM)
        a_out = _diff_attention(p["aq"], p["ak"], p["av"], bsz, t_pad, vec(a_lambda_q1), vec(a_lambda_k1),
                                vec(a_lambda_q2), vec(a_lambda_k2), a_subln_gain[layer].reshape(1, LANES), lambda_init)
        b_out = _dsa_attention(p["iq"], p["iw"], p["ik"], p["bq"], p["bk"], p["bv"], bsz, t_pad, k_sel)
        h1, hn, pq = _merge(h, a_out, b_out, p["ga"], p["gb"], w_branch_a[layer], w_branch_b[layer], w_out[layer],
                            ffn_norm_gain[layer], peer_w_query[layer])
        h = _peer(h1, hn, pq, peer_sub_keys[layer], peer_u[layer], peer_v[layer])

    return h.reshape(bsz, t_pad, d_model)[:, N_META_TOKENS:N_META_TOKENS + seq]
```

```python
import functools
import math

import jax
import jax.numpy as jnp
from jax import lax
from jax.experimental import pallas as pl
from jax.experimental.pallas import tpu as pltpu

F32 = jnp.float32
BF16 = jnp.bfloat16
I32 = jnp.int32

N_META_TOKENS = 16
SEQ_BLOCK = 128
ROPE_BASE = 10000.0
NORM_EPS = 1e-6
HEAD_DIM = 64
LANES = 128
A_HEADS = 4
B_HEADS = 8
IDX_HEADS = 8
IDX_ROPE = 32
TOPK_LIMIT = 256
PEER_HEADS = 8
PEER_TOPK = 16
MASKED_SCORE = -1e30
RUNNING_MAX_FLOOR = -1e29
INT_MIN = -(2 ** 31)
LOG2_E = math.log2(math.e)
VMEM_LIMIT = 56 * 1024 * 1024


def _pick_tile(n, candidates):
    for c in candidates:
        if n % c == 0:
            return c
    raise ValueError(f"no tile in {candidates} divides {n}")


def _params(semantics):
    return pltpu.CompilerParams(dimension_semantics=semantics, vmem_limit_bytes=VMEM_LIMIT)


def _dot(a, b):
    return jnp.dot(a, b, preferred_element_type=F32)


def _dot_nt(a, b):
    return lax.dot_general(a, b, (((1,), (1,)), ((), ())), preferred_element_type=F32)


def _lane_iota(shape):
    return lax.broadcasted_iota(I32, shape, 1)


_SEC = {}
_off = 0
for _name, _w in (("aq", 512), ("ak", 512), ("av", 512), ("bq", 512), ("bk", 128), ("bv", 128),
                  ("iq", 512), ("ik", 128), ("iw", 128), ("ga", 1024), ("gb", 1024)):
    _SEC[_name] = (_off, _off + _w)
    _off += _w
IN_COLS = _off


def _head_norm(x, gain, ones_blk):
    w = x.shape[1]
    x2 = x * x
    hi = x2.astype(BF16)
    lo = (x2 - hi.astype(F32)).astype(BF16)
    m = ones_blk[:w, :w]
    ms = _dot(hi, m) + _dot(lo, m)
    return x * lax.rsqrt(ms + NORM_EPS) * gain


def _rotary(x, cos, sin_signed, half):
    w = x.shape[1]
    lane = _lane_iota(x.shape)
    fwd = pltpu.roll(x, w - half, 1)
    bwd = pltpu.roll(x, half, 1)
    partner = jnp.where((lane & half) == 0, fwd, bwd)
    return x * cos + partner * sin_signed


def _inproj_kernel(h_ref, gain_ref, w_ref, ones_ref, cos_ref, sin_ref, icos_ref, isin_ref,
                   aqg_ref, akg_ref, bqg_ref, bkg_ref,
                   aq_o, ak_o, av_o, bq_o, bk_o, bv_o, iq_o, ik_o, iw_o, ga_o, gb_o, *, iw_scale):
    h = h_ref[...]
    xn = (h * lax.rsqrt(jnp.mean(h * h, axis=-1, keepdims=True) + NORM_EPS) * gain_ref[...]).astype(BF16)

    def proj(name):
        a, b = _SEC[name]
        return _dot(xn, w_ref[:, a:b])

    ones_blk = ones_ref[...]
    cos = cos_ref[...]
    sin = sin_ref[...]
    aq_o[...] = _rotary(_head_norm(proj("aq"), aqg_ref[...], ones_blk), cos, sin, 32).astype(BF16)
    ak_o[...] = _rotary(_head_norm(proj("ak"), akg_ref[...], ones_blk), cos, sin, 32).astype(BF16)
    av_o[...] = proj("av").astype(BF16)
    bq_o[...] = _rotary(_head_norm(proj("bq"), bqg_ref[...], ones_blk), cos, sin, 32).astype(BF16)
    bk_o[...] = _rotary(_head_norm(proj("bk"), bkg_ref[...], ones_blk), cos[:, :LANES], sin[:, :LANES], 32).astype(BF16)
    bv = proj("bv")
    bv_o[...] = jnp.where(_lane_iota(bv.shape) < HEAD_DIM, bv, 1.0).astype(BF16)
    icos = icos_ref[...]
    isin = isin_ref[...]
    iq_o[...] = _rotary(proj("iq"), icos, isin, 16).astype(BF16)
    ik_o[...] = _rotary(proj("ik"), icos[:, :LANES], isin[:, :LANES], 16).astype(BF16)
    iw_o[...] = proj("iw") * iw_scale
    ga_o[...] = jax.nn.sigmoid(proj("ga")).astype(BF16)
    gb_o[...] = jax.nn.sigmoid(proj("gb")).astype(BF16)


def _rope_tables(t_len, dim, width_dim):
    inv = ROPE_BASE ** (-jnp.arange(0, dim, 2, dtype=F32) / dim)
    ang = jnp.arange(t_len, dtype=F32)[:, None] * inv[None, :]
    cos, sin = jnp.cos(ang), jnp.sin(ang)
    rest = width_dim - dim
    cos_h = jnp.concatenate([cos, cos, jnp.ones((t_len, rest), F32)], axis=1)
    sin_h = jnp.concatenate([-sin, sin, jnp.zeros((t_len, rest), F32)], axis=1)
    return cos_h, sin_h


def _input_projection(h2d, t_pad, mix_gain, w_in, aqg, akg, bqg, bkg):
    n_tok, d_model = h2d.shape
    tm = _pick_tile(t_pad, (384, 256, 128))
    per_seq = t_pad // tm

    sizes = (512, 512, 512, 512, 64, 64, 512, 64, IDX_HEADS, d_model, d_model)
    offs = [0]
    for s in sizes:
        offs.append(offs[-1] + s)
    cols = [w_in[:, offs[i]:offs[i + 1]] for i in range(len(sizes))]
    waq, wak, wav, wbq, wbk, wbv, wiq, wik, wiw, wga, wgb = cols
    wiw = jnp.pad(wiw, ((0, 0), (0, LANES - IDX_HEADS)))
    w_all = jnp.concatenate([waq, wak, wav, wbq, wbk, wbk, wbv, wbv, wiq, wik, wik, wiw, wga, wgb],
                            axis=1).astype(BF16)
    assert w_all.shape[1] == IN_COLS

    cos_h, sin_h = _rope_tables(t_pad, HEAD_DIM, HEAD_DIM)
    cos = jnp.tile(cos_h, (1, 8))
    sin = jnp.tile(sin_h, (1, 8))
    icos_h, isin_h = _rope_tables(t_pad, IDX_ROPE, HEAD_DIM)
    icos = jnp.tile(icos_h, (1, 8))
    isin = jnp.tile(isin_h, (1, 8))
    head_id = jnp.arange(512) // HEAD_DIM
    ones_blk = jnp.where(head_id[:, None] == head_id[None, :], 1.0 / HEAD_DIM, 0.0).astype(BF16)

    row = lambda i: (i, 0)
    fixed = lambda i: (0, 0)
    tab = lambda i: (i % per_seq, 0)
    widths = dict(aq=512, ak=512, av=512, bq=512, bk=128, bv=128, iq=512, ik=128, iw=128, ga=d_model, gb=d_model)
    out_dtypes = dict(iw=F32)
    names = ("aq", "ak", "av", "bq", "bk", "bv", "iq", "ik", "iw", "ga", "gb")
    out_shape = tuple(jax.ShapeDtypeStruct((n_tok, widths[k]), out_dtypes.get(k, BF16)) for k in names)
    out_specs = tuple(pl.BlockSpec((tm, widths[k]), row) for k in names)
    in_specs = [
        pl.BlockSpec((tm, d_model), row),
        pl.BlockSpec((1, d_model), fixed),
        pl.BlockSpec((d_model, IN_COLS), fixed),
        pl.BlockSpec((512, 512), fixed),
        pl.BlockSpec((tm, 512), tab), pl.BlockSpec((tm, 512), tab),
        pl.BlockSpec((tm, 512), tab), pl.BlockSpec((tm, 512), tab),
        pl.BlockSpec((1, 512), fixed), pl.BlockSpec((1, 512), fixed),
        pl.BlockSpec((1, 512), fixed), pl.BlockSpec((1, LANES), fixed),
    ]
    iw_scale = IDX_HEADS ** -0.5 * HEAD_DIM ** -0.5
    outs = pl.pallas_call(
        functools.partial(_inproj_kernel, iw_scale=iw_scale),
        grid=(n_tok // tm,),
        in_specs=in_specs, out_specs=out_specs, out_shape=out_shape,
        compiler_params=_params(("parallel",)),
        name="inproj",
    )(h2d, mix_gain.reshape(1, d_model), w_all, ones_blk, cos, sin, icos, isin, aqg, akg, bqg, bkg)
    return dict(zip(names, outs))


def _stack_heads(x):
    rows, width = x.shape
    lane = _lane_iota((rows, LANES))
    zero = jnp.zeros((rows, LANES), x.dtype)
    groups = []
    for c in range(width // LANES):
        blk = x[:, c * LANES:(c + 1) * LANES]
        groups.append(jnp.where(lane < HEAD_DIM, blk, zero))
        groups.append(jnp.where(lane >= HEAD_DIM, blk, zero))
    return jnp.concatenate(groups, axis=0)


def _diff_kernel(q_ref, k_ref, v_ref, lq1_ref, lk1_ref, lq2_ref, lk2_ref, gain_ref, o_ref,
                 qs_ref, s_ref, p_ref, m_ref, l_ref, alpha_ref, acc_ref, *, tq, tk, lambda_init):
    i = pl.program_id(2)
    rows = 2 * tq
    lane_blocks = tk // LANES
    qs_ref[...] = _stack_heads(q_ref[...])
    m_ref[...] = jnp.full(m_ref.shape, RUNNING_MAX_FLOOR, F32)
    l_ref[...] = jnp.zeros(l_ref.shape, F32)
    acc_ref[...] = jnp.zeros(acc_ref.shape, F32)

    def body(j, _):
        start = pl.multiple_of(j * tk, tk)
        s_ref[...] = _dot_nt(qs_ref[...], k_ref[pl.ds(start, tk), :])
        for r0 in range(0, rows, SEQ_BLOCK):
            blk = slice(r0, r0 + SEQ_BLOCK)
            qpos = i * tq + r0 % tq + lax.broadcasted_iota(I32, (SEQ_BLOCK, 1), 0)
            s = jnp.where((_lane_iota((SEQ_BLOCK, tk)) + j * tk) <= qpos, s_ref[blk, :], MASKED_SCORE)
            m_prev = m_ref[blk, :]
            m_new = jnp.maximum(m_prev, jnp.max(s, axis=1, keepdims=True))
            alpha = jnp.exp2(m_prev - m_new)
            p = jnp.exp2(s - jnp.concatenate([m_new] * lane_blocks, axis=1))
            l_ref[blk, :] = alpha * l_ref[blk, :] + jnp.sum(p, axis=1, keepdims=True)
            alpha_ref[blk, :] = alpha
            p_ref[blk, :] = p.astype(BF16)
            m_ref[blk, :] = m_new
        acc_ref[...] = alpha_ref[...] * acc_ref[...] + _dot(p_ref[...], v_ref[pl.ds(start, tk), :])
        return 0

    n_chunks = ((i + 1) * tq + tk - 1) // tk
    lax.fori_loop(0, n_chunks, body, 0)
    o = acc_ref[...] / l_ref[...]
    lam =(jnp.exp(jnp.sum(lq1_ref[...] * lk1_ref[...], axis=-1, keepdims=True))
           - jnp.exp(jnp.sum(lq2_ref[...] * lk2_ref[...], axis=-1, keepdims=True)) + lambda_init)
    d = o[:tq] - lam * o[tq:]
    d = d * lax.rsqrt(jnp.mean(d * d, axis=-1, keepdims=True) + NORM_EPS) * gain_ref[...]
    o_ref[...] = (d * (1.0 - lambda_init)).astype(BF16)


def _diff_attention(aq, ak, av, bsz, t_pad, lq1, lk1, lq2, lk2, subln_gain, lambda_init):
    tq = _pick_tile(t_pad, (384, 256, 128))
    tk = tq
    nq = t_pad // tq
    qmap = lambda b, h, i: (b * nq + i, h)
    kvmap = lambda b, h, i: (b, h)
    fixed = lambda b, h, i: (0, 0)
    vec = pl.BlockSpec((1, HEAD_DIM), fixed)
    stats = pltpu.VMEM((2 * tq, LANES), F32)
    return pl.pallas_call(
        functools.partial(_diff_kernel, tq=tq, tk=tk, lambda_init=lambda_init),
        grid=(bsz, A_HEADS, nq),
        in_specs=[pl.BlockSpec((tq, LANES), qmap), pl.BlockSpec((t_pad, LANES), kvmap),
                  pl.BlockSpec((t_pad, LANES), kvmap), vec, vec, vec, vec, pl.BlockSpec((1, LANES), fixed)],
        out_specs=pl.BlockSpec((tq, LANES), qmap),
        out_shape=jax.ShapeDtypeStruct(aq.shape, BF16),
        scratch_shapes=[pltpu.VMEM((2 * tq, LANES), BF16), pltpu.VMEM((2 * tq, tk), F32),
                        pltpu.VMEM((2 * tq, tk), BF16), stats, stats, stats, stats],
        compiler_params=_params(("parallel", "parallel", "arbitrary")),
        name="diff_attn",
    )(aq, ak, av, lq1, lk1, lq2, lk2, subln_gain)


def _dsa_kernel(iq_ref, iw_ref, ik_ref, bq_ref, bk_ref, bv_ref, tri_ref, o_ref,
                key_ref, iqs_ref, qs_ref, st_ref, s_ref, p_ref, bias_ref, m_ref, alpha_ref, acc_ref, *, tq, tk, k_sel):
    i = pl.program_id(1)
    n_chunks = ((i + 1) * tq + tk - 1) // tk
    lane_blocks = tk // LANES
    qpos_row = i * tq + _lane_iota((1, tq))

    iqs_ref[...] = _stack_heads(iq_ref[...])
    iw_t = iw_ref[...].T

    def score_body(j, _):
        start = pl.multiple_of(j * tk, tk)
        st_ref[...] = _dot_nt(ik_ref[pl.ds(start, tk), :], iqs_ref[...])
        for r0 in range(0, tk, SEQ_BLOCK):
            score = jnp.zeros((SEQ_BLOCK, tq), F32)
            for h in range(IDX_HEADS):
                logits = st_ref[r0:r0 + SEQ_BLOCK, h * tq:(h + 1) * tq]
                score = score + jnp.maximum(logits, 0.0) * iw_t[h:h + 1, :]
            kpos = j * tk + r0 + lax.broadcasted_iota(I32, (SEQ_BLOCK, 1), 0)
            score = jnp.where(kpos <= qpos_row, score, -jnp.inf)
            bits = lax.bitcast_convert_type(score, I32)
            key_ref[pl.ds(start + r0, SEQ_BLOCK), :] = jnp.where(bits < 0, (bits ^ 0x7FFFFFFF) + 1, bits)
        return 0

    lax.fori_loop(0, n_chunks, score_body, 0)

    def count_ge(cand_row):
        cand = jnp.broadcast_to(cand_row, (8, tq))

        def body(j, acc):
            start = pl.multiple_of(j * tk, tk)
            for r0 in range(0, tk, 8):
                acc = acc + jnp.where(key_ref[pl.ds(start + r0, 8), :] >= cand, 1.0, 0.0)
            return acc

        acc = lax.fori_loop(0, n_chunks, body, jnp.zeros((8, tq), F32))
        return jnp.sum(acc, axis=0, keepdims=True)

    kf = float(k_sel)
    thr = jnp.where(count_ge(jnp.zeros((1, tq), I32)) >= kf, 0, INT_MIN).astype(I32)

    def bit_body(t, thr):
        cand = thr + jnp.left_shift(jnp.int32(1), 30 - t)
        return jnp.where(count_ge(cand) >= kf, cand, thr)

    thr = lax.fori_loop(0, 31, bit_body, thr)
    ties_kept = kf - count_ge(thr + 1)

    qs_ref[...] = _stack_heads(bq_ref[...])
    m_ref[...] = jnp.full(m_ref.shape, RUNNING_MAX_FLOOR, F32)
    acc_ref[...] = jnp.zeros(acc_ref.shape, F32)

    def attn_body(j, ties_before):
        start = pl.multiple_of(j * tk, tk)
        key = key_ref[pl.ds(start, tk), :]
        tie = key == thr
        tie_count = ties_before + _dot(tri_ref[...], jnp.where(tie, 1.0, 0.0).astype(BF16))
        kpos = j * tk + lax.broadcasted_iota(I32, (tk, 1), 0)
        sel = ((key > thr) | (tie & (tie_count <= ties_kept))) & (kpos <= qpos_row)
        bias_ref[...] = jnp.where(sel, 0.0, MASKED_SCORE).T
        s_ref[...] = _dot_nt(qs_ref[...], bk_ref[pl.ds(start, tk), :])
        for h in range(B_HEADS):
            for r0 in range(0, tq, SEQ_BLOCK):
                rows = slice(h * tq + r0, h * tq + r0 + SEQ_BLOCK)
                s = s_ref[rows, :] + bias_ref[r0:r0 + SEQ_BLOCK, :]
                m_prev = m_ref[rows, :]
                m_new = jnp.maximum(m_prev, jnp.max(s, axis=1, keepdims=True))
                alpha_ref[rows, :] = jnp.exp2(m_prev - m_new)
                p_ref[rows, :] = jnp.exp2(s - jnp.concatenate([m_new] * lane_blocks, axis=1)).astype(BF16)
                m_ref[rows, :] = m_new
        acc_ref[...] = alpha_ref[...] * acc_ref[...] + _dot(p_ref[...], bv_ref[pl.ds(start, tk), :])
        return tie_count[tk - 1:tk, :]

    lax.fori_loop(0, n_chunks, attn_body, jnp.zeros((1, tq), F32))
    lane = _lane_iota((tq, LANES))
    for c in range(B_HEADS // 2):
        even = acc_ref[(2 * c) * tq:(2 * c + 1) * tq, :]
        odd = acc_ref[(2 * c + 1) * tq:(2 * c + 2) * tq, :]
        o_ref[:, c * LANES:(c + 1) * LANES] = jnp.where(
            lane < HEAD_DIM, even / pltpu.roll(even, HEAD_DIM, 1), pltpu.roll(odd, HEAD_DIM, 1) / odd).astype(BF16)


def _dsa_attention(iq, iw, ik, bq, bk, bv, bsz, t_pad, k_sel):
    tq = _pick_tile(t_pad, (384, 256, 128))
    tk = tq
    nq = t_pad // tq
    qmap = lambda b, i: (b * nq + i, 0)
    kvmap = lambda b, i: (b, 0)
    tri = (jnp.arange(tk)[:, None] >= jnp.arange(tk)[None, :]).astype(BF16)
    stacked = pltpu.VMEM((B_HEADS * tq, LANES), BF16)
    stats = pltpu.VMEM((B_HEADS * tq, LANES), F32)
    return pl.pallas_call(
        functools.partial(_dsa_kernel, tq=tq, tk=tk, k_sel=k_sel),
        grid=(bsz, nq),
        in_specs=[pl.BlockSpec((tq, 512), qmap), pl.BlockSpec((tq, LANES), qmap), pl.BlockSpec((t_pad, LANES), kvmap),
                  pl.BlockSpec((tq, 512), qmap), pl.BlockSpec((t_pad, LANES), kvmap), pl.BlockSpec((t_pad, LANES), kvmap),
                  pl.BlockSpec((tk, tk), lambda b, i: (0, 0))],
        out_specs=pl.BlockSpec((tq, 512), qmap),
        out_shape=jax.ShapeDtypeStruct(bq.shape, BF16),
        scratch_shapes=[pltpu.VMEM((t_pad, tq), I32), stacked, stacked,
                        pltpu.VMEM((tk, B_HEADS * tq), F32), pltpu.VMEM((B_HEADS * tq, tk), F32), pltpu.VMEM((B_HEADS * tq, tk), BF16),
                        pltpu.VMEM((tq, tk), F32), stats, stats, stats],
        compiler_params=_params(("parallel", "arbitrary")),
        name="dsa_attn",
    )(iq, iw, ik, bq, bk, bv, tri)


def _merge_kernel(h_ref, a_ref, b_ref, ga_ref, gb_ref, wa_ref, wb_ref, wo_ref, gain_ref, wq_ref,
                  h1_o, hn_o, pq_o):
    ya = _dot(a_ref[...], wa_ref[...])
    yb = _dot(b_ref[...], wb_ref[...])
    mixed = ga_ref[...].astype(F32) * ya + gb_ref[...].astype(F32) * yb
    h1 = h_ref[...] + _dot(mixed.astype(BF16), wo_ref[...])
    h1_o[...] = h1
    hn = (h1 * lax.rsqrt(jnp.mean(h1 * h1, axis=-1, keepdims=True) + NORM_EPS) * gain_ref[...]).astype(BF16)
    hn_o[...] = hn
    pq_o[...] = _dot(hn, wq_ref[...]).astype(BF16)


def _merge(h2d, a_out, b_out, ga, gb, w_a, w_b, w_o, ffn_gain, w_query):
    n_tok, d_model = h2d.shape
    tm = _pick_tile(n_tok, (256, 128))
    row = lambda i: (i, 0)
    fixed = lambda i: (0, 0)
    pw = w_query.shape[1]
    return pl.pallas_call(
        _merge_kernel,
        grid=(n_tok // tm,),
        in_specs=[pl.BlockSpec((tm, d_model), row), pl.BlockSpec((tm, 512), row), pl.BlockSpec((tm, 512), row),
                  pl.BlockSpec((tm, d_model), row), pl.BlockSpec((tm, d_model), row),
                  pl.BlockSpec((512, d_model), fixed), pl.BlockSpec((512, d_model), fixed),
                  pl.BlockSpec((d_model, d_model), fixed), pl.BlockSpec((1, d_model), fixed),
                  pl.BlockSpec((d_model, pw), fixed)],
        out_specs=(pl.BlockSpec((tm, d_model), row), pl.BlockSpec((tm, d_model), row), pl.BlockSpec((tm, pw), row)),
        out_shape=(jax.ShapeDtypeStruct((n_tok, d_model), F32), jax.ShapeDtypeStruct((n_tok, d_model), BF16),
                   jax.ShapeDtypeStruct((n_tok, pw), BF16)),
        compiler_params=_params(("parallel",)),
        name="merge",
    )(h2d, a_out, b_out, ga, gb, w_a.astype(BF16), w_b.astype(BF16), w_o.astype(BF16),
      ffn_gain.reshape(1, d_model), w_query.astype(BF16))


def _top_values(x, count):
    vals = []
    cur = x
    rank = jnp.full(x.shape, float(count), F32)
    for it in range(count):
        m = jnp.max(cur, axis=0, keepdims=True)
        vals.append(m)
        hit = cur == m
        rank = jnp.where(hit, float(it), rank)
        cur = jnp.where(hit, -jnp.inf, cur)
    return jnp.concatenate(vals, axis=0), rank


def _pair_sum_candidates(v1, v2):
    k = v1.shape[0]
    pieces = []
    for a in range(k // 2):
        nb = min(k, -(-(k // (a + 1)) // 8) * 8)
        pieces.append(v1[a:a + 1] + v2[:nb])
    pieces.append(v1[k // 2:] + v2[0:1])
    return jnp.concatenate(pieces, axis=0)


def _peer_routing(pq_ref, sk_ref, cut_ref, e1_ref, rank_ref, e2_ref, n_keys):
    def head_body(h, _):
        lo = pl.multiple_of(h * LANES, LANES)
        st = _dot_nt(sk_ref[h], pq_ref[:, pl.ds(lo, LANES)])
        for c in range(st.shape[1] // LANES):
            cols = slice(c * LANES, (c + 1) * LANES)
            s1 = st[:n_keys, cols]
            s2 = st[n_keys:, cols]
            v1, _ = _top_values(s1, PEER_TOPK)
            v2, rank = _top_values(s2, PEER_TOPK)
            cand = _pair_sum_candidates(v1, v2)
            thr = _top_values(cand, PEER_TOPK)[0][PEER_TOPK - 1:PEER_TOPK]
            top = v1[0:1] + v2[0:1]
            z = jnp.sum(jnp.where(cand >= thr, jnp.exp(cand - top), 0.0), axis=0, keepdims=True)
            cut = jnp.zeros_like(s1)
            for b in range(PEER_TOPK):
                cut = cut + jnp.where(s1 + v2[b:b + 1] >= thr, 1.0, 0.0)
            cut_ref[h, c] = cut
            e1_ref[h, c] = jnp.exp(s1 - v1[0:1])
            rank_ref[h, c] = rank
            e2_ref[h, c] = jnp.exp(s2 - v2[0:1]) / z
        return 0

    lax.fori_loop(0, PEER_HEADS, head_body, 0)


def _gelu_exact(x):
    return 0.5 * x * (1.0 + lax.erf(x * (1.0 / math.sqrt(2.0))))


def _peer_kernel(h1_ref, hn_ref, pq_ref, sk_ref, u_ref, vt_ref, o_ref,
                 cut_ref, e1_ref, rank_ref, e2_ref, act_ref, p_ref, acc_ref, *, te, n_keys):
    s = pl.program_id(1)
    tm = hn_ref.shape[0]
    rows_per_tile = te // n_keys

    @pl.when(s == 0)
    def _():
        _peer_routing(pq_ref, sk_ref, cut_ref, e1_ref, rank_ref, e2_ref, n_keys)
        acc_ref[...] = jnp.zeros_like(acc_ref)

    act_ref[...] = _dot_nt(u_ref[...], hn_ref[...])

    def chunk_body(c, _):
        cols = pl.ds(pl.multiple_of(c * LANES, LANES), LANES)
        for row8 in range(rows_per_tile // 8):
            rows = pl.ds(pl.multiple_of(s * rows_per_tile + row8 * 8, 8), 8)
            cut8 = [cut_ref[h, c, rows, :] for h in range(PEER_HEADS)]
            e18 = [e1_ref[h, c, rows, :] for h in range(PEER_HEADS)]
            for k in range(8):
                row = row8 * 8 + k
                w = jnp.zeros((n_keys, LANES), F32)
                for h in range(PEER_HEADS):
                    cut_b = jnp.broadcast_to(cut8[h][k:k + 1], (n_keys, LANES))
                    e1_b = jnp.broadcast_to(e18[h][k:k + 1], (n_keys, LANES))
                    w = w + jnp.where(rank_ref[h, c] < cut_b, e2_ref[h, c], 0.0) * e1_b
                a = act_ref[row * n_keys:(row + 1) * n_keys, cols]
                p_ref[row * n_keys:(row + 1) * n_keys, cols] = (w * _gelu_exact(a)).astype(BF16)
        return 0

    lax.fori_loop(0, tm // LANES, chunk_body, 0)
    acc_ref[...] += _dot(vt_ref[...], p_ref[...])

    @pl.when(s == pl.num_programs(1) - 1)
    def _():
        o_ref[...] = h1_ref[...] + acc_ref[...].T


def _peer(h1, hn, pq, sub_keys, peer_u, peer_v):
    n_tok, d_model = h1.shape
    n_heads, _, n_keys, half = sub_keys.shape
    n_experts = peer_u.shape[0]
    assert n_heads == PEER_HEADS and n_keys == LANES and 2 * half == LANES and n_experts == n_keys * n_keys
    tm = _pick_tile(n_tok, (512, 384, 256, 128))
    te = 2048
    zeros = jnp.zeros((n_heads, n_keys, half), sub_keys.dtype)
    sk = jnp.concatenate([jnp.concatenate([sub_keys[:, 0], zeros], axis=2),
                          jnp.concatenate([zeros, sub_keys[:, 1]], axis=2)], axis=1).astype(BF16)
    u = peer_u.astype(BF16)
    vt = peer_v.astype(BF16).T
    row = lambda i, j: (i, 0)
    route = pltpu.VMEM((PEER_HEADS, tm // LANES, n_keys, LANES), F32)
    return pl.pallas_call(
        functools.partial(_peer_kernel, te=te, n_keys=n_keys),
        grid=(n_tok // tm, n_experts // te),
        in_specs=[pl.BlockSpec((tm, d_model), row), pl.BlockSpec((tm, d_model), row),
                  pl.BlockSpec((tm, PEER_HEADS * LANES), row),
                  pl.BlockSpec((n_heads, 2 * n_keys, LANES), lambda i, s: (0, 0, 0)),
                  pl.BlockSpec((te, d_model), lambda i, s: (s, 0)),
                  pl.BlockSpec((d_model, te), lambda i, s: (0, s))],
        out_specs=pl.BlockSpec((tm, d_model), row),
        out_shape=jax.ShapeDtypeStruct((n_tok, d_model), F32),
        scratch_shapes=[route, route, route, route, pltpu.VMEM((te, tm), F32),
                        pltpu.VMEM((te, tm), BF16), pltpu.VMEM((d_model, tm), F32)],
        compiler_params=_params(("parallel", "arbitrary")),
        name="peer",
    )(h1, hn, pq, sk, u, vt)


def kernel(x, meta_tokens, mix_norm_gain, w_in, a_q_norm_gain, a_k_norm_gain, a_lambda_q1, a_lambda_k1, a_lambda_q2, a_lambda_k2, a_subln_gain, w_branch_a, b_q_norm_gain, b_k_norm_gain, w_branch_b, w_out, ffn_norm_gain, peer_w_query, peer_sub_keys, peer_u, peer_v):
    bsz, seq, d_model = x.shape
    depth = w_in.shape[0]
    t_real = N_META_TOKENS + seq
    t_pad = ((t_real + SEQ_BLOCK - 1) // SEQ_BLOCK) * SEQ_BLOCK
    k_sel = min(TOPK_LIMIT, seq // 4)

    meta = jnp.broadcast_to(meta_tokens.astype(x.dtype)[None], (bsz, N_META_TOKENS, d_model))
    pad = jnp.zeros((bsz, t_pad - t_real, d_model), x.dtype)
    h = jnp.concatenate([meta, x, pad], axis=1).reshape(bsz * t_pad, d_model)

    q_scale = HEAD_DIM ** -0.5 * LOG2_E
    for layer in range(depth):
        lambda_init = 0.8 - 0.6 * math.exp(-0.3 * layer)
        aqg = (jnp.tile(a_q_norm_gain[layer], 8) * q_scale).reshape(1, 512)
        akg = jnp.tile(a_k_norm_gain[layer], 8).reshape(1, 512)
        bqg = (jnp.tile(b_q_norm_gain[layer], 8) * q_scale).reshape(1, 512)
        bkg = jnp.tile(b_k_norm_gain[layer], 2).reshape(1, LANES)
        p = _input_projection(h, t_pad, mix_norm_gain[layer], w_in[layer], aqg, akg, bqg, bkg)
        vec = lambda a: a[layer].astype(F32).reshape(1, HEAD_DIM)
        a_out = _diff_attention(p["aq"], p["ak"], p["av"], bsz, t_pad, vec(a_lambda_q1), vec(a_lambda_k1),
                                vec(a_lambda_q2), vec(a_lambda_k2), a_subln_gain[layer].reshape(1, LANES), lambda_init)
        b_out = _dsa_attention(p["iq"], p["iw"], p["ik"], p["bq"], p["bk"], p["bv"], bsz, t_pad, k_sel)
        h1, hn, pq = _merge(h, a_out, b_out, p["ga"], p["gb"], w_branch_a[layer], w_branch_b[layer], w_out[layer],
                            ffn_norm_gain[layer], peer_w_query[layer])
        h = _peer(h1, hn, pq, peer_sub_keys[layer], peer_u[layer], peer_v[layer])

    return h.reshape(bsz, t_pad, d_model)[:, N_META_TOKENS:N_META_TOKENS + seq]
```

```python
import functools
import math

import jax
import jax.numpy as jnp
from jax import lax
from jax.experimental import pallas as pl
from jax.experimental.pallas import tpu as pltpu

F32 = jnp.float32
BF16 = jnp.bfloat16
I32 = jnp.int32

N_META_TOKENS = 16
SEQ_BLOCK = 128
ROPE_BASE = 10000.0
NORM_EPS = 1e-6
HEAD_DIM = 64
LANES = 128
A_HEADS = 4
B_HEADS = 8
IDX_HEADS = 8
IDX_ROPE = 32
TOPK_LIMIT = 256
PEER_HEADS = 8
PEER_TOPK = 16
MASKED_SCORE = -1e30
RUNNING_MAX_FLOOR = -1e29
INT_MIN = -(2 ** 31)
LOG2_E = math.log2(math.e)
VMEM_LIMIT = 56 * 1024 * 1024


def _pick_tile(n, candidates):
    for c in candidates:
        if n % c == 0:
            return c
    raise ValueError(f"no tile in {candidates} divides {n}")


def _params(semantics):
    return pltpu.CompilerParams(dimension_semantics=semantics, vmem_limit_bytes=VMEM_LIMIT)


def _dot(a, b):
    return jnp.dot(a, b, preferred_element_type=F32)


def _dot_nt(a, b):
    return lax.dot_general(a, b, (((1,), (1,)), ((), ())), preferred_element_type=F32)


def _lane_iota(shape):
    return lax.broadcasted_iota(I32, shape, 1)


_SEC = {}
_off = 0
for _name, _w in (("aq", 512), ("ak", 512), ("av", 512), ("bq", 512), ("bk", 128), ("bv", 128),
                  ("iq", 512), ("ik", 128), ("iw", 128), ("ga", 1024), ("gb", 1024)):
    _SEC[_name] = (_off, _off + _w)
    _off += _w
IN_COLS = _off


def _head_norm(x, gain, ones_blk):
    w = x.shape[1]
    x2 = x * x
    hi = x2.astype(BF16)
    lo = (x2 - hi.astype(F32)).astype(BF16)
    m = ones_blk[:w, :w]
    ms = _dot(hi, m) + _dot(lo, m)
    return x * lax.rsqrt(ms + NORM_EPS) * gain


def _rotary(x, cos, sin_signed, half):
    w = x.shape[1]
    lane = _lane_iota(x.shape)
    fwd = pltpu.roll(x, w - half, 1)
    bwd = pltpu.roll(x, half, 1)
    partner = jnp.where((lane & half) == 0, fwd, bwd)
    return x * cos + partner * sin_signed


def _inproj_kernel(h_ref, gain_ref, w_ref, ones_ref, cos_ref, sin_ref, icos_ref, isin_ref,
                   aqg_ref, akg_ref, bqg_ref, bkg_ref,
                   aq_o, ak_o, av_o, bq_o, bk_o, bv_o, iq_o, ik_o, iw_o, ga_o, gb_o, *, iw_scale):
    h = h_ref[...]
    xn = (h * lax.rsqrt(jnp.mean(h * h, axis=-1, keepdims=True) + NORM_EPS) * gain_ref[...]).astype(BF16)

    def proj(name):
        a, b = _SEC[name]
        return _dot(xn, w_ref[:, a:b])

    ones_blk = ones_ref[...]
    cos = cos_ref[...]
    sin = sin_ref[...]
    aq_o[...] = _rotary(_head_norm(proj("aq"), aqg_ref[...], ones_blk), cos, sin, 32).astype(BF16)
    ak_o[...] = _rotary(_head_norm(proj("ak"), akg_ref[...], ones_blk), cos, sin, 32).astype(BF16)
    av_o[...] = proj("av").astype(BF16)
    bq_o[...] = _rotary(_head_norm(proj("bq"), bqg_ref[...], ones_blk), cos, sin, 32).astype(BF16)
    bk_o[...] = _rotary(_head_norm(proj("bk"), bkg_ref[...], ones_blk), cos[:, :LANES], sin[:, :LANES], 32).astype(BF16)
    bv = proj("bv")
    bv_o[...] = jnp.where(_lane_iota(bv.shape) < HEAD_DIM, bv, 1.0).astype(BF16)
    icos = icos_ref[...]
    isin = isin_ref[...]
    iq_o[...] = _rotary(proj("iq"), icos, isin, 16).astype(BF16)
    ik_o[...] = _rotary(proj("ik"), icos[:, :LANES], isin[:, :LANES], 16).astype(BF16)
    iw_o[...] = proj("iw") * iw_scale
    ga_o[...] = jax.nn.sigmoid(proj("ga")).astype(BF16)
    gb_o[...] = jax.nn.sigmoid(proj("gb")).astype(BF16)


def _rope_tables(t_len, dim, width_dim):
    inv = ROPE_BASE ** (-jnp.arange(0, dim, 2, dtype=F32) / dim)
    ang = jnp.arange(t_len, dtype=F32)[:, None] * inv[None, :]
    cos, sin = jnp.cos(ang), jnp.sin(ang)
    rest = width_dim - dim
    cos_h = jnp.concatenate([cos, cos, jnp.ones((t_len, rest), F32)], axis=1)
    sin_h = jnp.concatenate([-sin, sin, jnp.zeros((t_len, rest), F32)], axis=1)
    return cos_h, sin_h


def _input_projection(h2d, t_pad, mix_gain, w_in, aqg, akg, bqg, bkg):
    n_tok, d_model = h2d.shape
    tm = _pick_tile(t_pad, (384, 256, 128))
    per_seq = t_pad // tm

    sizes = (512, 512, 512, 512, 64, 64, 512, 64, IDX_HEADS, d_model, d_model)
    offs = [0]
    for s in sizes:
        offs.append(offs[-1] + s)
    cols = [w_in[:, offs[i]:offs[i + 1]] for i in range(len(sizes))]
    waq, wak, wav, wbq, wbk, wbv, wiq, wik, wiw, wga, wgb = cols
    wiw = jnp.pad(wiw, ((0, 0), (0, LANES - IDX_HEADS)))
    w_all = jnp.concatenate([waq, wak, wav, wbq, wbk, wbk, wbv, wbv, wiq, wik, wik, wiw, wga, wgb],
                            axis=1).astype(BF16)
    assert w_all.shape[1] == IN_COLS

    cos_h, sin_h = _rope_tables(t_pad, HEAD_DIM, HEAD_DIM)
    cos = jnp.tile(cos_h, (1, 8))
    sin = jnp.tile(sin_h, (1, 8))
    icos_h, isin_h = _rope_tables(t_pad, IDX_ROPE, HEAD_DIM)
    icos = jnp.tile(icos_h, (1, 8))
    isin = jnp.tile(isin_h, (1, 8))
    head_id = jnp.arange(512) // HEAD_DIM
    ones_blk = jnp.where(head_id[:, None] == head_id[None, :], 1.0 / HEAD_DIM, 0.0).astype(BF16)

    row = lambda i: (i, 0)
    fixed = lambda i: (0, 0)
    tab = lambda i: (i % per_seq, 0)
    widths = dict(aq=512, ak=512, av=512, bq=512, bk=128, bv=128, iq=512, ik=128, iw=128, ga=d_model, gb=d_model)
    out_dtypes = dict(iw=F32)
    names = ("aq", "ak", "av", "bq", "bk", "bv", "iq", "ik", "iw", "ga", "gb")
    out_shape = tuple(jax.ShapeDtypeStruct((n_tok, widths[k]), out_dtypes.get(k, BF16)) for k in names)
    out_specs = tuple(pl.BlockSpec((tm, widths[k]), row) for k in names)
    in_specs = [
        pl.BlockSpec((tm, d_model), row),
        pl.BlockSpec((1, d_model), fixed),
        pl.BlockSpec((d_model, IN_COLS), fixed),
        pl.BlockSpec((512, 512), fixed),
        pl.BlockSpec((tm, 512), tab), pl.BlockSpec((tm, 512), tab),
        pl.BlockSpec((tm, 512), tab), pl.BlockSpec((tm, 512), tab),
        pl.BlockSpec((1, 512), fixed), pl.BlockSpec((1, 512), fixed),
        pl.BlockSpec((1, 512), fixed), pl.BlockSpec((1, LANES), fixed),
    ]
    iw_scale = IDX_HEADS ** -0.5 * HEAD_DIM ** -0.5
    outs = pl.pallas_call(
        functools.partial(_inproj_kernel, iw_scale=iw_scale),
        grid=(n_tok // tm,),
        in_specs=in_specs, out_specs=out_specs, out_shape=out_shape,
        compiler_params=_params(("parallel",)),
        name="inproj",
    )(h2d, mix_gain.reshape(1, d_model), w_all, ones_blk, cos, sin, icos, isin, aqg, akg, bqg, bkg)
    return dict(zip(names, outs))


def _stack_heads(x):
    rows, width = x.shape
    lane = _lane_iota((rows, LANES))
    zero = jnp.zeros((rows, LANES), x.dtype)
    groups = []
    for c in range(width // LANES):
        blk = x[:, c * LANES:(c + 1) * LANES]
        groups.append(jnp.where(lane < HEAD_DIM, blk, zero))
        groups.append(jnp.where(lane >= HEAD_DIM, blk, zero))
    return jnp.concatenate(groups, axis=0)


def _diff_kernel(q_ref, k_ref, v_ref, lq1_ref, lk1_ref, lq2_ref, lk2_ref, gain_ref, o_ref,
                 qs_ref, s_ref, p_ref, m_ref, l_ref, alpha_ref, acc_ref, *, tq, tk, lambda_init):
    assert tq == tk
    i = pl.program_id(1)
    rows = 2 * tq
    lane_blocks = tk // LANES
    heads = [(h, slice(h * LANES, (h + 1) * LANES)) for h in range(A_HEADS)]
    for h, cols in heads:
        qs_ref[h] = _stack_heads(q_ref[:, cols])
    m_ref[...] = jnp.full(m_ref.shape, RUNNING_MAX_FLOOR, F32)
    l_ref[...] = jnp.zeros(l_ref.shape, F32)
    acc_ref[...] = jnp.zeros(acc_ref.shape, F32)
    blocks = [slice(r0, r0 + SEQ_BLOCK) for r0 in range(0, rows, SEQ_BLOCK)]

    def chunk(j, on_diagonal):
        start = pl.multiple_of(j * tk, tk)
        for h, cols in heads:
            s_ref[h] = _dot_nt(qs_ref[h], k_ref[pl.ds(start, tk), cols])
        for h, _ in heads:
            for blk in blocks:
                s = s_ref[h, blk, :]
                if on_diagonal:
                    qpos = blk.start % tq + lax.broadcasted_iota(I32, (SEQ_BLOCK, 1), 0)
                    s = jnp.where(_lane_iota((SEQ_BLOCK, tk)) <= qpos, s, MASKED_SCORE)
                    s_ref[h, blk, :] = s
                m_prev = m_ref[h, blk, :]
                m_new = jnp.maximum(m_prev, jnp.max(s, axis=1, keepdims=True))
                alpha_ref[h, blk, :] = jnp.exp2(m_prev - m_new)
                m_ref[h, blk, :] = m_new
        for h, cols in heads:
            for blk in blocks:
                p = jnp.exp2(s_ref[h, blk, :] - jnp.concatenate([m_ref[h, blk, :]] * lane_blocks, axis=1))
                l_ref[h, blk, :] = alpha_ref[h, blk, :] * l_ref[h, blk, :] + jnp.sum(p, axis=1, keepdims=True)
                p_ref[h, blk, :] = p.astype(BF16)
            acc_ref[h] = alpha_ref[h] * acc_ref[h] + _dot(p_ref[h], v_ref[pl.ds(start, tk), cols])

    def body(j, _):
        chunk(j, on_diagonal=False)
        return 0

    lax.fori_loop(0, i, body, 0)
    chunk(i, on_diagonal=True)
    lam = (jnp.exp(jnp.sum(lq1_ref[...] * lk1_ref[...], axis=-1, keepdims=True))
           - jnp.exp(jnp.sum(lq2_ref[...] * lk2_ref[...], axis=-1, keepdims=True)) + lambda_init)
    for h, cols in heads:
        o = acc_ref[h] / l_ref[h]
        d = o[:tq] - lam * o[tq:]
        d = d * lax.rsqrt(jnp.mean(d * d, axis=-1, keepdims=True) + NORM_EPS) * gain_ref[...]
        o_ref[:, cols] = (d * (1.0 - lambda_init)).astype(BF16)


def _diff_attention(aq, ak, av, bsz, t_pad, lq1, lk1, lq2, lk2, subln_gain, lambda_init):
    tq = _pick_tile(t_pad, (384, 256, 128))
    tk = tq
    nq = t_pad // tq
    width = A_HEADS * LANES
    qmap = lambda b, i: (b * nq + i, 0)
    kvmap = lambda b, i: (b, 0)
    fixed = lambda b, i: (0, 0)
    vec = pl.BlockSpec((1, HEAD_DIM), fixed)
    stats = pltpu.VMEM((A_HEADS, 2 * tq, LANES), F32)
    return pl.pallas_call(
        functools.partial(_diff_kernel, tq=tq, tk=tk, lambda_init=lambda_init),
        grid=(bsz, nq),
        in_specs=[pl.BlockSpec((tq, width), qmap), pl.BlockSpec((t_pad, width), kvmap),
                  pl.BlockSpec((t_pad, width), kvmap), vec, vec, vec, vec, pl.BlockSpec((1, LANES), fixed)],
        out_specs=pl.BlockSpec((tq, width), qmap),
        out_shape=jax.ShapeDtypeStruct(aq.shape, BF16),
        scratch_shapes=[pltpu.VMEM((A_HEADS, 2 * tq, LANES), BF16), pltpu.VMEM((A_HEADS, 2 * tq, tk), F32),
                        pltpu.VMEM((A_HEADS, 2 * tq, tk), BF16), stats, stats, stats, stats],
        compiler_params=_params(("parallel", "arbitrary")),
        name="diff_attn",
    )(aq, ak, av, lq1, lk1, lq2, lk2, subln_gain)


def _dsa_kernel(iq_ref, iw_ref, ik_ref, bq_ref, bk_ref, bv_ref, tri_ref, o_ref,
                key_ref, iqs_ref, qs_ref, st_ref, s_ref, p_ref, bias_ref, m_ref, alpha_ref, acc_ref, *, tq, tk, k_sel):
    i = pl.program_id(1)
    n_chunks = ((i + 1) * tq + tk - 1) // tk
    lane_blocks = tk // LANES
    qpos_row = i * tq + _lane_iota((1, tq))

    iqs_ref[...] = _stack_heads(iq_ref[...])
    iw_t = iw_ref[...].T

    def score_body(j, _):
        start = pl.multiple_of(j * tk, tk)
        st_ref[...] = _dot_nt(ik_ref[pl.ds(start, tk), :], iqs_ref[...])
        for r0 in range(0, tk, SEQ_BLOCK):
            score = jnp.zeros((SEQ_BLOCK, tq), F32)
            for h in range(IDX_HEADS):
                logits = st_ref[r0:r0 + SEQ_BLOCK, h * tq:(h + 1) * tq]
                score = score + jnp.maximum(logits, 0.0) * iw_t[h:h + 1, :]
            kpos = j * tk + r0 + lax.broadcasted_iota(I32, (SEQ_BLOCK, 1), 0)
            score = jnp.where(kpos <= qpos_row, score, -jnp.inf)
            bits = lax.bitcast_convert_type(score, I32)
            key_ref[pl.ds(start + r0, SEQ_BLOCK), :] = jnp.where(bits < 0, (bits ^ 0x7FFFFFFF) + 1, bits)
        return 0

    lax.fori_loop(0, n_chunks, score_body, 0)

    def count_ge(cand_row):
        cand = jnp.broadcast_to(cand_row, (8, tq))

        def body(j, acc):
            start = pl.multiple_of(j * tk, tk)
            for r0 in range(0, tk, 8):
                acc = acc + jnp.where(key_ref[pl.ds(start + r0, 8), :] >= cand, 1.0, 0.0)
            return acc

        acc = lax.fori_loop(0, n_chunks, body, jnp.zeros((8, tq), F32))
        return jnp.sum(acc, axis=0, keepdims=True)

    kf = float(k_sel)
    thr = jnp.where(count_ge(jnp.zeros((1, tq), I32)) >= kf, 0, INT_MIN).astype(I32)

    def bit_body(t, thr):
        cand = thr + jnp.left_shift(jnp.int32(1), 30 - t)
        return jnp.where(count_ge(cand) >= kf, cand, thr)

    thr = lax.fori_loop(0, 31, bit_body, thr)
    ties_kept = kf - count_ge(thr + 1)

    qs_ref[...] = _stack_heads(bq_ref[...])
    m_ref[...] = jnp.full(m_ref.shape, RUNNING_MAX_FLOOR, F32)
    acc_ref[...] = jnp.zeros(acc_ref.shape, F32)

    def attn_body(j, ties_before):
        start = pl.multiple_of(j * tk, tk)
        key = key_ref[pl.ds(start, tk), :]
        tie = key == thr
        tie_count = ties_before + _dot(tri_ref[...], jnp.where(tie, 1.0, 0.0).astype(BF16))
        kpos = j * tk + lax.broadcasted_iota(I32, (tk, 1), 0)
        sel = ((key > thr) | (tie & (tie_count <= ties_kept))) & (kpos <= qpos_row)
        bias_ref[...] = jnp.where(sel, 0.0, MASKED_SCORE).T
        s_ref[...] = _dot_nt(qs_ref[...], bk_ref[pl.ds(start, tk), :])
        blocks = [(slice(h * tq + r0, h * tq + r0 + SEQ_BLOCK), slice(r0, r0 + SEQ_BLOCK))
                  for h in range(B_HEADS) for r0 in range(0, tq, SEQ_BLOCK)]
        for rows, qrows in blocks:
            s = s_ref[rows, :] + bias_ref[qrows, :]
            s_ref[rows, :] = s
            m_prev = m_ref[rows, :]
            m_new = jnp.maximum(m_prev, jnp.max(s, axis=1, keepdims=True))
            alpha_ref[rows, :] = jnp.exp2(m_prev - m_new)
            m_ref[rows, :] = m_new
        for rows, _ in blocks:
            m_new = jnp.concatenate([m_ref[rows, :]] * lane_blocks, axis=1)
            p_ref[rows, :] = jnp.exp2(s_ref[rows, :] - m_new).astype(BF16)
        acc_ref[...] = alpha_ref[...] * acc_ref[...] + _dot(p_ref[...], bv_ref[pl.ds(start, tk), :])
        return tie_count[tk - 1:tk, :]

    lax.fori_loop(0, n_chunks, attn_body, jnp.zeros((1, tq), F32))
    lane = _lane_iota((tq, LANES))
    for c in range(B_HEADS // 2):
        even = acc_ref[(2 * c) * tq:(2 * c + 1) * tq, :]
        odd = acc_ref[(2 * c + 1) * tq:(2 * c + 2) * tq, :]
        o_ref[:, c * LANES:(c + 1) * LANES] = jnp.where(
            lane < HEAD_DIM, even / pltpu.roll(even, HEAD_DIM, 1), pltpu.roll(odd, HEAD_DIM, 1) / odd).astype(BF16)


def _dsa_attention(iq, iw, ik, bq, bk, bv, bsz, t_pad, k_sel):
    tq = _pick_tile(t_pad, (384, 256, 128))
    tk = tq
    nq = t_pad // tq
    qmap = lambda b, i: (b * nq + i, 0)
    kvmap = lambda b, i: (b, 0)
    tri = (jnp.arange(tk)[:, None] >= jnp.arange(tk)[None, :]).astype(BF16)
    stacked = pltpu.VMEM((B_HEADS * tq, LANES), BF16)
    stats = pltpu.VMEM((B_HEADS * tq, LANES), F32)
    return pl.pallas_call(
        functools.partial(_dsa_kernel, tq=tq, tk=tk, k_sel=k_sel),
        grid=(bsz, nq),
        in_specs=[pl.BlockSpec((tq, 512), qmap), pl.BlockSpec((tq, LANES), qmap), pl.BlockSpec((t_pad, LANES), kvmap),
                  pl.BlockSpec((tq, 512), qmap), pl.BlockSpec((t_pad, LANES), kvmap), pl.BlockSpec((t_pad, LANES), kvmap),
                  pl.BlockSpec((tk, tk), lambda b, i: (0, 0))],
        out_specs=pl.BlockSpec((tq, 512), qmap),
        out_shape=jax.ShapeDtypeStruct(bq.shape, BF16),
        scratch_shapes=[pltpu.VMEM((t_pad, tq), I32), stacked, stacked,
                        pltpu.VMEM((tk, B_HEADS * tq), F32), pltpu.VMEM((B_HEADS * tq, tk), F32), pltpu.VMEM((B_HEADS * tq, tk), BF16),
                        pltpu.VMEM((tq, tk), F32), stats, stats, stats],
        compiler_params=_params(("parallel", "arbitrary")),
        name="dsa_attn",
    )(iq, iw, ik, bq, bk, bv, tri)


def _merge_kernel(h_ref, a_ref, b_ref, ga_ref, gb_ref, wa_ref, wb_ref, wo_ref, gain_ref, wq_ref,
                  h1_o, hn_o, pq_o):
    ya = _dot(a_ref[...], wa_ref[...])
    yb = _dot(b_ref[...], wb_ref[...])
    mixed = ga_ref[...].astype(F32) * ya + gb_ref[...].astype(F32) * yb
    h1 = h_ref[...] + _dot(mixed.astype(BF16), wo_ref[...])
    h1_o[...] = h1
    hn = (h1 * lax.rsqrt(jnp.mean(h1 * h1, axis=-1, keepdims=True) + NORM_EPS) * gain_ref[...]).astype(BF16)
    hn_o[...] = hn
    pq_o[...] = _dot(hn, wq_ref[...]).astype(BF16)


def _merge(h2d, a_out, b_out, ga, gb, w_a, w_b, w_o, ffn_gain, w_query):
    n_tok, d_model = h2d.shape
    tm = _pick_tile(n_tok, (256, 128))
    row = lambda i: (i, 0)
    fixed = lambda i: (0, 0)
    pw = w_query.shape[1]
    return pl.pallas_call(
        _merge_kernel,
        grid=(n_tok // tm,),
        in_specs=[pl.BlockSpec((tm, d_model), row), pl.BlockSpec((tm, 512), row), pl.BlockSpec((tm, 512), row),
                  pl.BlockSpec((tm, d_model), row), pl.BlockSpec((tm, d_model), row),
                  pl.BlockSpec((512, d_model), fixed), pl.BlockSpec((512, d_model), fixed),
                  pl.BlockSpec((d_model, d_model), fixed), pl.BlockSpec((1, d_model), fixed),
                  pl.BlockSpec((d_model, pw), fixed)],
        out_specs=(pl.BlockSpec((tm, d_model), row), pl.BlockSpec((tm, d_model), row), pl.BlockSpec((tm, pw), row)),
        out_shape=(jax.ShapeDtypeStruct((n_tok, d_model), F32), jax.ShapeDtypeStruct((n_tok, d_model), BF16),
                   jax.ShapeDtypeStruct((n_tok, pw), BF16)),
        compiler_params=_params(("parallel",)),
        name="merge",
    )(h2d, a_out, b_out, ga, gb, w_a.astype(BF16), w_b.astype(BF16), w_o.astype(BF16),
      ffn_gain.reshape(1, d_model), w_query.astype(BF16))


def _top_values(x, count):
    vals = []
    cur = x
    rank = jnp.full(x.shape, float(count), F32)
    for it in range(count):
        m = jnp.max(cur, axis=0, keepdims=True)
        vals.append(m)
        hit = cur == m
        rank = jnp.where(hit, float(it), rank)
        cur = jnp.where(hit, -jnp.inf, cur)
    return jnp.concatenate(vals, axis=0), rank


def _pair_sum_candidates(v1, v2):
    k = v1.shape[0]
    pieces = []
    for a in range(k // 2):
        nb = min(k, -(-(k // (a + 1)) // 8) * 8)
        pieces.append(v1[a:a + 1] + v2[:nb])
    pieces.append(v1[k // 2:] + v2[0:1])
    return jnp.concatenate(pieces, axis=0)


def _peer_routing(pq_ref, sk_ref, cut_ref, e1_ref, rank_ref, e2_ref, n_keys):
    def head_body(h, _):
        lo = pl.multiple_of(h * LANES, LANES)
        st = _dot_nt(sk_ref[h], pq_ref[:, pl.ds(lo, LANES)])
        for c in range(st.shape[1] // LANES):
            cols = slice(c * LANES, (c + 1) * LANES)
            s1 = st[:n_keys, cols]
            s2 = st[n_keys:, cols]
            v1, _ = _top_values(s1, PEER_TOPK)
            v2, rank = _top_values(s2, PEER_TOPK)
            cand = _pair_sum_candidates(v1, v2)
            thr = _top_values(cand, PEER_TOPK)[0][PEER_TOPK - 1:PEER_TOPK]
            top = v1[0:1] + v2[0:1]
            z = jnp.sum(jnp.where(cand >= thr, jnp.exp(cand - top), 0.0), axis=0, keepdims=True)
            cut = jnp.zeros_like(s1)
            for b in range(PEER_TOPK):
                cut = cut + jnp.where(s1 + v2[b:b + 1] >= thr, 1.0, 0.0)
            cut_ref[h, c] = cut
            e1_ref[h, c] = jnp.exp(s1 - v1[0:1])
            rank_ref[h, c] = rank
            e2_ref[h, c] = jnp.exp(s2 - v2[0:1]) / z
        return 0

    lax.fori_loop(0, PEER_HEADS, head_body, 0)


def _gelu_exact(x):
    return 0.5 * x * (1.0 + lax.erf(x * (1.0 / math.sqrt(2.0))))


def _peer_kernel(h1_ref, hn_ref, pq_ref, sk_ref, u_ref, vt_ref, o_ref,
                 cut_ref, e1_ref, rank_ref, e2_ref, act_ref, p_ref, acc_ref, *, te, n_keys):
    s = pl.program_id(1)
    tm = hn_ref.shape[0]
    rows_per_tile = te // n_keys

    @pl.when(s == 0)
    def _():
        _peer_routing(pq_ref, sk_ref, cut_ref, e1_ref, rank_ref, e2_ref, n_keys)
        acc_ref[...] = jnp.zeros_like(acc_ref)

    act_ref[...] = _dot_nt(u_ref[...], hn_ref[...])

    def chunk_body(c, _):
        cols = pl.ds(pl.multiple_of(c * LANES, LANES), LANES)
        for row8 in range(rows_per_tile // 8):
            rows = pl.ds(pl.multiple_of(s * rows_per_tile + row8 * 8, 8), 8)
            cut8 = [cut_ref[h, c, rows, :] for h in range(PEER_HEADS)]
            e18 = [e1_ref[h, c, rows, :] for h in range(PEER_HEADS)]
            for k in range(8):
                row = row8 * 8 + k
                w = jnp.zeros((n_keys, LANES), F32)
                for h in range(PEER_HEADS):
                    cut_b = jnp.broadcast_to(cut8[h][k:k + 1], (n_keys, LANES))
                    e1_b = jnp.broadcast_to(e18[h][k:k + 1], (n_keys, LANES))
                    w = w + jnp.where(rank_ref[h, c] < cut_b, e2_ref[h, c], 0.0) * e1_b
                a = act_ref[row * n_keys:(row + 1) * n_keys, cols]
                p_ref[row * n_keys:(row + 1) * n_keys, cols] = (w * _gelu_exact(a)).astype(BF16)
        return 0

    lax.fori_loop(0, tm // LANES, chunk_body, 0)
    acc_ref[...] += _dot(vt_ref[...], p_ref[...])

    @pl.when(s == pl.num_programs(1) - 1)
    def _():
        o_ref[...] = h1_ref[...] + acc_ref[...].T


def _peer(h1, hn, pq, sub_keys, peer_u, peer_v):
    n_tok, d_model = h1.shape
    n_heads, _, n_keys, half = sub_keys.shape
    n_experts = peer_u.shape[0]
    assert n_heads == PEER_HEADS and n_keys == LANES and 2 * half == LANES and n_experts == n_keys * n_keys
    tm = _pick_tile(n_tok, (512, 384, 256, 128))
    te = 2048
    zeros = jnp.zeros((n_heads, n_keys, half), sub_keys.dtype)
    sk = jnp.concatenate([jnp.concatenate([sub_keys[:, 0], zeros], axis=2),
                          jnp.concatenate([zeros, sub_keys[:, 1]], axis=2)], axis=1).astype(BF16)
    u = peer_u.astype(BF16)
    vt = peer_v.astype(BF16).T
    row = lambda i, j: (i, 0)
    route = pltpu.VMEM((PEER_HEADS, tm // LANES, n_keys, LANES), F32)
    return pl.pallas_call(
        functools.partial(_peer_kernel, te=te, n_keys=n_keys),
        grid=(n_tok // tm, n_experts // te),
        in_specs=[pl.BlockSpec((tm, d_model), row), pl.BlockSpec((tm, d_model), row),
                  pl.BlockSpec((tm, PEER_HEADS * LANES), row),
                  pl.BlockSpec((n_heads, 2 * n_keys, LANES), lambda i, s: (0, 0, 0)),
                  pl.BlockSpec((te, d_model), lambda i, s: (s, 0)),
                  pl.BlockSpec((d_model, te), lambda i, s: (0, s))],
        out_specs=pl.BlockSpec((tm, d_model), row),
        out_shape=jax.ShapeDtypeStruct((n_tok, d_model), F32),
        scratch_shapes=[route, route, route, route, pltpu.VMEM((te, tm), F32),
                        pltpu.VMEM((te, tm), BF16), pltpu.VMEM((d_model, tm), F32)],
        compiler_params=_params(("parallel", "arbitrary")),
        name="peer",
    )(h1, hn, pq, sk, u, vt)


def kernel(x, meta_tokens, mix_norm_gain, w_in, a_q_norm_gain, a_k_norm_gain, a_lambda_q1, a_lambda_k1, a_lambda_q2, a_lambda_k2, a_subln_gain, w_branch_a, b_q_norm_gain, b_k_norm_gain, w_branch_b, w_out, ffn_norm_gain, peer_w_query, peer_sub_keys, peer_u, peer_v):
    bsz, seq, d_model = x.shape
    depth = w_in.shape[0]
    t_real = N_META_TOKENS + seq
    t_pad = ((t_real + SEQ_BLOCK - 1) // SEQ_BLOCK) * SEQ_BLOCK
    k_sel = min(TOPK_LIMIT, seq // 4)

    meta = jnp.broadcast_to(meta_tokens.astype(x.dtype)[None], (bsz, N_META_TOKENS, d_model))
    pad = jnp.zeros((bsz, t_pad - t_real, d_model), x.dtype)
    h = jnp.concatenate([meta, x, pad], axis=1).reshape(bsz * t_pad, d_model)

    q_scale = HEAD_DIM ** -0.5 * LOG2_E
    for layer in range(depth):
        lambda_init = 0.8 - 0.6 * math.exp(-0.3 * layer)
        aqg = (jnp.tile(a_q_norm_gain[layer], 8) * q_scale).reshape(1, 512)
        akg = jnp.tile(a_k_norm_gain[layer], 8).reshape(1, 512)
        bqg = (jnp.tile(b_q_norm_gain[layer], 8) * q_scale).reshape(1, 512)
        bkg = jnp.tile(b_k_norm_gain[layer], 2).reshape(1, LANES)
        p = _input_projection(h, t_pad, mix_norm_gain[layer], w_in[layer], aqg, akg, bqg, bkg)
        vec = lambda a: a[layer].astype(F32).reshape(1, HEAD_DIM)
        a_out = _diff_attention(p["aq"], p["ak"], p["av"], bsz, t_pad, vec(a_lambda_q1), vec(a_lambda_k1),
                                vec(a_lambda_q2), vec(a_lambda_k2), a_subln_gain[layer].reshape(1, LANES), lambda_init)
        b_out = _dsa_attention(p["iq"], p["iw"], p["ik"], p["bq"], p["bk"], p["bv"], bsz, t_pad, k_sel)
        h1, hn, pq = _merge(h, a_out, b_out, p["ga"], p["gb"], w_branch_a[layer], w_branch_b[layer], w_out[layer],
                            ffn_norm_gain[layer], peer_w_query[layer])
        h = _peer(h1, hn, pq, peer_sub_keys[layer], peer_u[layer], peer_v[layer])

    return h.reshape(bsz, t_pad, d_model)[:, N_META_TOKENS:N_META_TOKENS + seq]
```

```python
import functools
import math

import jax
import jax.numpy as jnp
from jax import lax
from jax.experimental import pallas as pl
from jax.experimental.pallas import tpu as pltpu

F32 = jnp.float32
BF16 = jnp.bfloat16
I32 = jnp.int32

N_META_TOKENS = 16
SEQ_BLOCK = 128
ROPE_BASE = 10000.0
NORM_EPS = 1e-6
HEAD_DIM = 64
LANES = 128
A_HEADS = 4
B_HEADS = 8
IDX_HEADS = 8
IDX_ROPE = 32
TOPK_LIMIT = 256
PEER_HEADS = 8
PEER_TOPK = 16
MASKED_SCORE = -1e30
RUNNING_MAX_FLOOR = -1e29
INT_MIN = -(2 ** 31)
LOG2_E = math.log2(math.e)
INV_SQRT2 = 1.0 / math.sqrt(2.0)
VMEM_LIMIT = 56 * 1024 * 1024


def _pick_tile(n, candidates):
    for c in candidates:
        if n % c == 0:
            return c
    raise ValueError(f"no tile in {candidates} divides {n}")


def _params(semantics):
    return pltpu.CompilerParams(dimension_semantics=semantics, vmem_limit_bytes=VMEM_LIMIT)


def _dot(a, b):
    return jnp.dot(a, b, preferred_element_type=F32)


def _dot_nt(a, b):
    return lax.dot_general(a, b, (((1,), (1,)), ((), ())), preferred_element_type=F32)


def _lane_iota(shape):
    return lax.broadcasted_iota(I32, shape, 1)


_SEC = {}
_off = 0
for _name, _w in (("aq", 512), ("ak", 512), ("av", 512), ("bq", 512), ("bk", 128), ("bv", 128),
                  ("iq", 512), ("ik", 128), ("iw", 128), ("ga", 1024), ("gb", 1024)):
    _SEC[_name] = (_off, _off + _w)
    _off += _w
IN_COLS = _off


def _head_norm(x, gain, ones_blk):
    w = x.shape[1]
    x2 = x * x
    hi = x2.astype(BF16)
    lo = (x2 - hi.astype(F32)).astype(BF16)
    m = ones_blk[:w, :w]
    ms = _dot(hi, m) + _dot(lo, m)
    return x * lax.rsqrt(ms + NORM_EPS) * gain


def _rotary(x, cos, sin_signed, half):
    w = x.shape[1]
    lane = _lane_iota(x.shape)
    fwd = pltpu.roll(x, w - half, 1)
    bwd = pltpu.roll(x, half, 1)
    partner = jnp.where((lane & half) == 0, fwd, bwd)
    return x * cos + partner * sin_signed


def _inproj_kernel(h_ref, gain_ref, w_ref, ones_ref, cos_ref, sin_ref, icos_ref, isin_ref,
                   aqg_ref, akg_ref, bqg_ref, bkg_ref,
                   aq_o, ak_o, av_o, bq_o, bk_o, bv_o, iq_o, ik_o, iw_o, ga_o, gb_o, *, iw_scale):
    h = h_ref[...]
    xn = (h * lax.rsqrt(jnp.mean(h * h, axis=-1, keepdims=True) + NORM_EPS) * gain_ref[...]).astype(BF16)

    def proj(name):
        a, b = _SEC[name]
        return _dot(xn, w_ref[:, a:b])

    ones_blk = ones_ref[...]
    cos = cos_ref[...]
    sin = sin_ref[...]
    aq_o[...] = _rotary(_head_norm(proj("aq"), aqg_ref[...], ones_blk), cos, sin, 32).astype(BF16)
    ak_o[...] = _rotary(_head_norm(proj("ak"), akg_ref[...], ones_blk), cos, sin, 32).astype(BF16)
    av_o[...] = proj("av").astype(BF16)
    bq_o[...] = _rotary(_head_norm(proj("bq"), bqg_ref[...], ones_blk), cos, sin, 32).astype(BF16)
    bk_o[...] = _rotary(_head_norm(proj("bk"), bkg_ref[...], ones_blk), cos[:, :LANES], sin[:, :LANES], 32).astype(BF16)
    bv = proj("bv")
    bv_o[...] = jnp.where(_lane_iota(bv.shape) < HEAD_DIM, bv, 1.0).astype(BF16)
    icos = icos_ref[...]
    isin = isin_ref[...]
    iq_o[...] = _rotary(proj("iq"), icos, isin, 16).astype(BF16)
    ik_o[...] = _rotary(proj("ik"), icos[:, :LANES], isin[:, :LANES], 16).astype(BF16)
    iw_o[...] = proj("iw") * iw_scale
    ga_o[...] = jax.nn.sigmoid(proj("ga")).astype(BF16)
    gb_o[...] = jax.nn.sigmoid(proj("gb")).astype(BF16)


def _rope_tables(t_len, dim, width_dim):
    inv = ROPE_BASE ** (-jnp.arange(0, dim, 2, dtype=F32) / dim)
    ang = jnp.arange(t_len, dtype=F32)[:, None] * inv[None, :]
    cos, sin = jnp.cos(ang), jnp.sin(ang)
    rest = width_dim - dim
    cos_h = jnp.concatenate([cos, cos, jnp.ones((t_len, rest), F32)], axis=1)
    sin_h = jnp.concatenate([-sin, sin, jnp.zeros((t_len, rest), F32)], axis=1)
    return cos_h, sin_h


def _input_projection(h2d, t_pad, mix_gain, w_in, aqg, akg, bqg, bkg):
    n_tok, d_model = h2d.shape
    tm = _pick_tile(t_pad, (384, 256, 128))
    per_seq = t_pad // tm

    sizes = (512, 512, 512, 512, 64, 64, 512, 64, IDX_HEADS, d_model, d_model)
    offs = [0]
    for s in sizes:
        offs.append(offs[-1] + s)
    cols = [w_in[:, offs[i]:offs[i + 1]] for i in range(len(sizes))]
    waq, wak, wav, wbq, wbk, wbv, wiq, wik, wiw, wga, wgb = cols
    wiw = jnp.pad(wiw, ((0, 0), (0, LANES - IDX_HEADS)))
    w_all = jnp.concatenate([waq, wak, wav, wbq, wbk, wbk, wbv, wbv, wiq, wik, wik, wiw, wga, wgb],
                            axis=1).astype(BF16)
    assert w_all.shape[1] == IN_COLS

    cos_h, sin_h = _rope_tables(t_pad, HEAD_DIM, HEAD_DIM)
    cos = jnp.tile(cos_h, (1, 8))
    sin = jnp.tile(sin_h, (1, 8))
    icos_h, isin_h = _rope_tables(t_pad, IDX_ROPE, HEAD_DIM)
    icos = jnp.tile(icos_h, (1, 8))
    isin = jnp.tile(isin_h, (1, 8))
    head_id = jnp.arange(512) // HEAD_DIM
    ones_blk = jnp.where(head_id[:, None] == head_id[None, :], 1.0 / HEAD_DIM, 0.0).astype(BF16)

    row = lambda i: (i, 0)
    fixed = lambda i: (0, 0)
    tab = lambda i: (i % per_seq, 0)
    widths = dict(aq=512, ak=512, av=512, bq=512, bk=128, bv=128, iq=512, ik=128, iw=128, ga=d_model, gb=d_model)
    out_dtypes = dict(iw=F32)
    names = ("aq", "ak", "av", "bq", "bk", "bv", "iq", "ik", "iw", "ga", "gb")
    out_shape = tuple(jax.ShapeDtypeStruct((n_tok, widths[k]), out_dtypes.get(k, BF16)) for k in names)
    out_specs = tuple(pl.BlockSpec((tm, widths[k]), row) for k in names)
    in_specs = [
        pl.BlockSpec((tm, d_model), row),
        pl.BlockSpec((1, d_model), fixed),
        pl.BlockSpec((d_model, IN_COLS), fixed),
        pl.BlockSpec((512, 512), fixed),
        pl.BlockSpec((tm, 512), tab), pl.BlockSpec((tm, 512), tab),
        pl.BlockSpec((tm, 512), tab), pl.BlockSpec((tm, 512), tab),
        pl.BlockSpec((1, 512), fixed), pl.BlockSpec((1, 512), fixed),
        pl.BlockSpec((1, 512), fixed), pl.BlockSpec((1, LANES), fixed),
    ]
    iw_scale = IDX_HEADS ** -0.5 * HEAD_DIM ** -0.5
    outs = pl.pallas_call(
        functools.partial(_inproj_kernel, iw_scale=iw_scale),
        grid=(n_tok // tm,),
        in_specs=in_specs, out_specs=out_specs, out_shape=out_shape,
        compiler_params=_params(("parallel",)),
        name="inproj",
    )(h2d, mix_gain.reshape(1, d_model), w_all, ones_blk, cos, sin, icos, isin, aqg, akg, bqg, bkg)
    return dict(zip(names, outs))


def _stack_heads(x):
    rows, width = x.shape
    lane = _lane_iota((rows, LANES))
    zero = jnp.zeros((rows, LANES), x.dtype)
    groups = []
    for c in range(width // LANES):
        blk = x[:, c * LANES:(c + 1) * LANES]
        groups.append(jnp.where(lane < HEAD_DIM, blk, zero))
        groups.append(jnp.where(lane >= HEAD_DIM, blk, zero))
    return jnp.concatenate(groups, axis=0)


def _diff_kernel(q_ref, k_ref, v_ref, lq1_ref, lk1_ref, lq2_ref, lk2_ref, gain_ref, o_ref,
                 qs_ref, s_ref, p_ref, m_ref, l_ref, alpha_ref, acc_ref, *, tq, tk, lambda_init):
    assert tq == tk
    i = pl.program_id(1)
    rows = 2 * tq
    lane_blocks = tk // LANES
    heads = [(h, slice(h * LANES, (h + 1) * LANES)) for h in range(A_HEADS)]
    for h, cols in heads:
        qs_ref[h] = _stack_heads(q_ref[:, cols])
    m_ref[...] = jnp.full(m_ref.shape, RUNNING_MAX_FLOOR, F32)
    l_ref[...] = jnp.zeros(l_ref.shape, F32)
    acc_ref[...] = jnp.zeros(acc_ref.shape, F32)
    blocks = [slice(r0, r0 + SEQ_BLOCK) for r0 in range(0, rows, SEQ_BLOCK)]

    def chunk(j, on_diagonal):
        start = pl.multiple_of(j * tk, tk)
        for h, cols in heads:
            s_ref[h] = _dot_nt(qs_ref[h], k_ref[pl.ds(start, tk), cols])
        for h, _ in heads:
            for blk in blocks:
                s = s_ref[h, blk, :]
                if on_diagonal:
                    qpos = blk.start % tq + lax.broadcasted_iota(I32, (SEQ_BLOCK, 1), 0)
                    s = jnp.where(_lane_iota((SEQ_BLOCK, tk)) <= qpos, s, MASKED_SCORE)
                    s_ref[h, blk, :] = s
                m_prev = m_ref[h, blk, :]
                m_new = jnp.maximum(m_prev, jnp.max(s, axis=1, keepdims=True))
                alpha_ref[h, blk, :] = jnp.exp2(m_prev - m_new)
                m_ref[h, blk, :] = m_new
        for h, cols in heads:
            for blk in blocks:
                p = jnp.exp2(s_ref[h, blk, :] - jnp.concatenate([m_ref[h, blk, :]] * lane_blocks, axis=1))
                l_ref[h, blk, :] = alpha_ref[h, blk, :] * l_ref[h, blk, :] + jnp.sum(p, axis=1, keepdims=True)
                p_ref[h, blk, :] = p.astype(BF16)
            acc_ref[h] = alpha_ref[h] * acc_ref[h] + _dot(p_ref[h], v_ref[pl.ds(start, tk), cols])

    def body(j, _):
        chunk(j, on_diagonal=False)
        return 0

    lax.fori_loop(0, i, body, 0)
    chunk(i, on_diagonal=True)
    lam = (jnp.exp(jnp.sum(lq1_ref[...] * lk1_ref[...], axis=-1, keepdims=True))
           - jnp.exp(jnp.sum(lq2_ref[...] * lk2_ref[...], axis=-1, keepdims=True)) + lambda_init)
    for h, cols in heads:
        o = acc_ref[h] / l_ref[h]
        d = o[:tq] - lam * o[tq:]
        d = d * lax.rsqrt(jnp.mean(d * d, axis=-1, keepdims=True) + NORM_EPS) * gain_ref[...]
        o_ref[:, cols] = (d * (1.0 - lambda_init)).astype(BF16)


def _diff_attention(aq, ak, av, bsz, t_pad, lq1, lk1, lq2, lk2, subln_gain, lambda_init):
    tq = _pick_tile(t_pad, (384, 256, 128))
    tk = tq
    nq = t_pad // tq
    width = A_HEADS * LANES
    qmap = lambda b, i: (b * nq + i, 0)
    kvmap = lambda b, i: (b, 0)
    fixed = lambda b, i: (0, 0)
    vec = pl.BlockSpec((1, HEAD_DIM), fixed)
    stats = pltpu.VMEM((A_HEADS, 2 * tq, LANES), F32)
    return pl.pallas_call(
        functools.partial(_diff_kernel, tq=tq, tk=tk, lambda_init=lambda_init),
        grid=(bsz, nq),
        in_specs=[pl.BlockSpec((tq, width), qmap), pl.BlockSpec((t_pad, width), kvmap),
                  pl.BlockSpec((t_pad, width), kvmap), vec, vec, vec, vec, pl.BlockSpec((1, LANES), fixed)],
        out_specs=pl.BlockSpec((tq, width), qmap),
        out_shape=jax.ShapeDtypeStruct(aq.shape, BF16),
        scratch_shapes=[pltpu.VMEM((A_HEADS, 2 * tq, LANES), BF16), pltpu.VMEM((A_HEADS, 2 * tq, tk), F32),
                        pltpu.VMEM((A_HEADS, 2 * tq, tk), BF16), stats, stats, stats, stats],
        compiler_params=_params(("parallel", "arbitrary")),
        name="diff_attn",
    )(aq, ak, av, lq1, lk1, lq2, lk2, subln_gain)


def _dsa_kernel(iq_ref, iw_ref, ik_ref, bq_ref, bk_ref, bv_ref, tri_ref, o_ref,
                key_ref, iqs_ref, qs_ref, st_ref, s_ref, p_ref, bias_ref, m_ref, alpha_ref, acc_ref, *, tq, tk, k_sel):
    i = pl.program_id(1)
    n_chunks = ((i + 1) * tq + tk - 1) // tk
    lane_blocks = tk // LANES
    qpos_row = i * tq + _lane_iota((1, tq))

    iqs_ref[...] = _stack_heads(iq_ref[...])
    iw_t = iw_ref[...].T

    def score_body(j, _):
        start = pl.multiple_of(j * tk, tk)
        st_ref[...] = _dot_nt(ik_ref[pl.ds(start, tk), :], iqs_ref[...])
        for r0 in range(0, tk, SEQ_BLOCK):
            score = jnp.zeros((SEQ_BLOCK, tq), F32)
            for h in range(IDX_HEADS):
                logits = st_ref[r0:r0 + SEQ_BLOCK, h * tq:(h + 1) * tq]
                score = score + jnp.maximum(logits, 0.0) * iw_t[h:h + 1, :]
            kpos = j * tk + r0 + lax.broadcasted_iota(I32, (SEQ_BLOCK, 1), 0)
            score = jnp.where(kpos <= qpos_row, score, -jnp.inf)
            bits = lax.bitcast_convert_type(score, I32)
            key_ref[pl.ds(start + r0, SEQ_BLOCK), :] = jnp.where(bits < 0, (bits ^ 0x7FFFFFFF) + 1, bits)
        return 0

    lax.fori_loop(0, n_chunks, score_body, 0)

    def count_ge(cand_row):
        cand = jnp.broadcast_to(cand_row, (8, tq))

        def body(j, acc):
            start = pl.multiple_of(j * tk, tk)
            for r0 in range(0, tk, 8):
                acc = acc + jnp.where(key_ref[pl.ds(start + r0, 8), :] >= cand, 1.0, 0.0)
            return acc

        acc = lax.fori_loop(0, n_chunks, body, jnp.zeros((8, tq), F32))
        return jnp.sum(acc, axis=0, keepdims=True)

    kf = float(k_sel)
    thr = jnp.where(count_ge(jnp.zeros((1, tq), I32)) >= kf, 0, INT_MIN).astype(I32)

    def bit_body(t, thr):
        cand = thr + jnp.left_shift(jnp.int32(1), 30 - t)
        return jnp.where(count_ge(cand) >= kf, cand, thr)

    thr = lax.fori_loop(0, 31, bit_body, thr)
    ties_kept = kf - count_ge(thr + 1)

    qs_ref[...] = _stack_heads(bq_ref[...])
    m_ref[...] = jnp.full(m_ref.shape, RUNNING_MAX_FLOOR, F32)
    acc_ref[...] = jnp.zeros(acc_ref.shape, F32)

    def attn_body(j, ties_before):
        start = pl.multiple_of(j * tk, tk)
        key = key_ref[pl.ds(start, tk), :]
        tie = key == thr
        tie_count = ties_before + _dot(tri_ref[...], jnp.where(tie, 1.0, 0.0).astype(BF16))
        kpos = j * tk + lax.broadcasted_iota(I32, (tk, 1), 0)
        sel = ((key > thr) | (tie & (tie_count <= ties_kept))) & (kpos <= qpos_row)
        bias_ref[...] = jnp.where(sel, 0.0, MASKED_SCORE).T
        s_ref[...] = _dot_nt(qs_ref[...], bk_ref[pl.ds(start, tk), :])
        blocks = [(slice(h * tq + r0, h * tq + r0 + SEQ_BLOCK), slice(r0, r0 + SEQ_BLOCK))
                  for h in range(B_HEADS) for r0 in range(0, tq, SEQ_BLOCK)]
        for rows, qrows in blocks:
            s = s_ref[rows, :] + bias_ref[qrows, :]
            s_ref[rows, :] = s
            m_prev = m_ref[rows, :]
            m_new = jnp.maximum(m_prev, jnp.max(s, axis=1, keepdims=True))
            alpha_ref[rows, :] = jnp.exp2(m_prev - m_new)
            m_ref[rows, :] = m_new
        for rows, _ in blocks:
            m_new = jnp.concatenate([m_ref[rows, :]] * lane_blocks, axis=1)
            p_ref[rows, :] = jnp.exp2(s_ref[rows, :] - m_new).astype(BF16)
        acc_ref[...] = alpha_ref[...] * acc_ref[...] + _dot(p_ref[...], bv_ref[pl.ds(start, tk), :])
        return tie_count[tk - 1:tk, :]

    lax.fori_loop(0, n_chunks, attn_body, jnp.zeros((1, tq), F32))
    lane = _lane_iota((tq, LANES))
    for c in range(B_HEADS // 2):
        even = acc_ref[(2 * c) * tq:(2 * c + 1) * tq, :]
        odd = acc_ref[(2 * c + 1) * tq:(2 * c + 2) * tq, :]
        o_ref[:, c * LANES:(c + 1) * LANES] = jnp.where(
            lane < HEAD_DIM, even / pltpu.roll(even, HEAD_DIM, 1), pltpu.roll(odd, HEAD_DIM, 1) / odd).astype(BF16)


def _dsa_attention(iq, iw, ik, bq, bk, bv, bsz, t_pad, k_sel):
    tq = _pick_tile(t_pad, (384, 256, 128))
    tk = tq
    nq = t_pad // tq
    qmap = lambda b, i: (b * nq + i, 0)
    kvmap = lambda b, i: (b, 0)
    tri = (jnp.arange(tk)[:, None] >= jnp.arange(tk)[None, :]).astype(BF16)
    stacked = pltpu.VMEM((B_HEADS * tq, LANES), BF16)
    stats = pltpu.VMEM((B_HEADS * tq, LANES), F32)
    return pl.pallas_call(
        functools.partial(_dsa_kernel, tq=tq, tk=tk, k_sel=k_sel),
        grid=(bsz, nq),
        in_specs=[pl.BlockSpec((tq, 512), qmap), pl.BlockSpec((tq, LANES), qmap), pl.BlockSpec((t_pad, LANES), kvmap),
                  pl.BlockSpec((tq, 512), qmap), pl.BlockSpec((t_pad, LANES), kvmap), pl.BlockSpec((t_pad, LANES), kvmap),
                  pl.BlockSpec((tk, tk), lambda b, i: (0, 0))],
        out_specs=pl.BlockSpec((tq, 512), qmap),
        out_shape=jax.ShapeDtypeStruct(bq.shape, BF16),
        scratch_shapes=[pltpu.VMEM((t_pad, tq), I32), stacked, stacked,
                        pltpu.VMEM((tk, B_HEADS * tq), F32), pltpu.VMEM((B_HEADS * tq, tk), F32), pltpu.VMEM((B_HEADS * tq, tk), BF16),
                        pltpu.VMEM((tq, tk), F32), stats, stats, stats],
        compiler_params=_params(("parallel", "arbitrary")),
        name="dsa_attn",
    )(iq, iw, ik, bq, bk, bv, tri)


def _merge_kernel(h_ref, a_ref, b_ref, ga_ref, gb_ref, wa_ref, wb_ref, wo_ref, gain_ref, wq_ref,
                  h1_o, hn_o, pq_o):
    ya = _dot(a_ref[...], wa_ref[...])
    yb = _dot(b_ref[...], wb_ref[...])
    mixed = ga_ref[...].astype(F32) * ya + gb_ref[...].astype(F32) * yb
    h1 = h_ref[...] + _dot(mixed.astype(BF16), wo_ref[...])
    h1_o[...] = h1
    hn = h1 * lax.rsqrt(jnp.mean(h1 * h1, axis=-1, keepdims=True) + NORM_EPS) * gain_ref[...]
    hn_o[...] = (hn * INV_SQRT2).astype(BF16)
    pq_o[...] = _dot(hn.astype(BF16), wq_ref[...]).astype(BF16)


def _merge(h2d, a_out, b_out, ga, gb, w_a, w_b, w_o, ffn_gain, w_query):
    n_tok, d_model = h2d.shape
    tm = _pick_tile(n_tok, (256, 128))
    row = lambda i: (i, 0)
    fixed = lambda i: (0, 0)
    pw = w_query.shape[1]
    return pl.pallas_call(
        _merge_kernel,
        grid=(n_tok // tm,),
        in_specs=[pl.BlockSpec((tm, d_model), row), pl.BlockSpec((tm, 512), row), pl.BlockSpec((tm, 512), row),
                  pl.BlockSpec((tm, d_model), row), pl.BlockSpec((tm, d_model), row),
                  pl.BlockSpec((512, d_model), fixed), pl.BlockSpec((512, d_model), fixed),
                  pl.BlockSpec((d_model, d_model), fixed), pl.BlockSpec((1, d_model), fixed),
                  pl.BlockSpec((d_model, pw), fixed)],
        out_specs=(pl.BlockSpec((tm, d_model), row), pl.BlockSpec((tm, d_model), row), pl.BlockSpec((tm, pw), row)),
        out_shape=(jax.ShapeDtypeStruct((n_tok, d_model), F32), jax.ShapeDtypeStruct((n_tok, d_model), BF16),
                   jax.ShapeDtypeStruct((n_tok, pw), BF16)),
        compiler_params=_params(("parallel",)),
        name="merge",
    )(h2d, a_out, b_out, ga, gb, w_a.astype(BF16), w_b.astype(BF16), w_o.astype(BF16),
      ffn_gain.reshape(1, d_model), w_query.astype(BF16))


def _top_values(xs, count):
    curs = list(xs)
    ranks = [jnp.full(x.shape, float(count), F32) for x in xs]
    vals = [[] for _ in xs]
    for it in range(count):
        for k in range(len(curs)):
            m = jnp.max(curs[k], axis=0, keepdims=True)
            vals[k].append(m)
            hit = curs[k] == m
            ranks[k] = jnp.where(hit, float(it), ranks[k])
            curs[k] = jnp.where(hit, -jnp.inf, curs[k])
    return [(jnp.concatenate(v, axis=0), r) for v, r in zip(vals, ranks)]


def _pair_sum_candidates(v1, v2):
    k = v1.shape[0]
    pieces = []
    for a in range(k // 2):
        nb = min(k, -(-(k // (a + 1)) // 8) * 8)
        pieces.append(v1[a:a + 1] + v2[:nb])
    pieces.append(v1[k // 2:] + v2[0:1])
    return jnp.concatenate(pieces, axis=0)


def _peer_routing(pq_ref, sk_ref, cut_ref, e1_ref, rank_ref, e2_ref, n_keys):
    def head_body(h, _):
        lo = pl.multiple_of(h * LANES, LANES)
        st = _dot_nt(sk_ref[h], pq_ref[:, pl.ds(lo, LANES)])
        n_blocks = st.shape[1] // LANES
        s1 = [st[:n_keys, c * LANES:(c + 1) * LANES] for c in range(n_blocks)]
        s2 = [st[n_keys:, c * LANES:(c + 1) * LANES] for c in range(n_blocks)]
        tops = _top_values(s1 + s2, PEER_TOPK)
        v1 = [t[0] for t in tops[:n_blocks]]
        v2 = [t[0] for t in tops[n_blocks:]]
        rank = [t[1] for t in tops[n_blocks:]]
        cand = [_pair_sum_candidates(a, b) for a, b in zip(v1, v2)]
        thr = [t[0][PEER_TOPK - 1:PEER_TOPK] for t in _top_values(cand, PEER_TOPK)]
        for c in range(n_blocks):
            top = v1[c][0:1] + v2[c][0:1]
            z = jnp.sum(jnp.where(cand[c] >= thr[c], jnp.exp(cand[c] - top), 0.0), axis=0, keepdims=True)
            cut = jnp.zeros_like(s1[c])
            for b in range(PEER_TOPK):
                cut = cut + jnp.where(s1[c] + v2[c][b:b + 1] >= thr[c], 1.0, 0.0)
            cut_ref[h, c] = cut
            e1_ref[h, c] = jnp.exp(s1[c] - v1[c][0:1]) * INV_SQRT2
            rank_ref[h, c] = rank[c]
            e2_ref[h, c] = jnp.exp(s2[c] - v2[c][0:1]) / z
        return 0

    lax.fori_loop(0, PEER_HEADS, head_body, 0)


def _gelu_scaled(y):
    return y * (1.0 + lax.erf(y))


def _peer_kernel(h1_ref, hn_ref, pq_ref, sk_ref, u_ref, vt_ref, o_ref,
                 cut_ref, e1_ref, rank_ref, e2_ref, act_ref, p_ref, acc_ref, *, te, n_keys):
    s = pl.program_id(1)
    tm = hn_ref.shape[0]
    rows_per_tile = te // n_keys

    @pl.when(s == 0)
    def _():
        _peer_routing(pq_ref, sk_ref, cut_ref, e1_ref, rank_ref, e2_ref, n_keys)
        acc_ref[...] = jnp.zeros_like(acc_ref)

    act_ref[...] = _dot_nt(u_ref[...], hn_ref[...])

    def chunk_body(c, _):
        cols = pl.ds(pl.multiple_of(c * LANES, LANES), LANES)
        for row8 in range(rows_per_tile // 8):
            rows = pl.ds(pl.multiple_of(s * rows_per_tile + row8 * 8, 8), 8)
            cut8 = [cut_ref[h, c, rows, :] for h in range(PEER_HEADS)]
            e18 = [e1_ref[h, c, rows, :] for h in range(PEER_HEADS)]
            for k in range(8):
                row = row8 * 8 + k
                w = jnp.zeros((n_keys, LANES), F32)
                for h in range(PEER_HEADS):
                    cut_b = jnp.broadcast_to(cut8[h][k:k + 1], (n_keys, LANES))
                    e1_b = jnp.broadcast_to(e18[h][k:k + 1], (n_keys, LANES))
                    w = w + jnp.where(rank_ref[h, c] < cut_b, e2_ref[h, c], 0.0) * e1_b
                a = act_ref[row * n_keys:(row + 1) * n_keys, cols]
                p_ref[row * n_keys:(row + 1) * n_keys, cols] = (w * _gelu_scaled(a)).astype(BF16)
        return 0

    lax.fori_loop(0, tm // LANES, chunk_body, 0)
    acc_ref[...] += _dot(vt_ref[...], p_ref[...])

    @pl.when(s == pl.num_programs(1) - 1)
    def _():
        o_ref[...] = h1_ref[...] + acc_ref[...].T


def _peer(h1, hn, pq, sub_keys, peer_u, peer_v):
    n_tok, d_model = h1.shape
    n_heads, _, n_keys, half = sub_keys.shape
    n_experts = peer_u.shape[0]
    assert n_heads == PEER_HEADS and n_keys == LANES and 2 * half == LANES and n_experts == n_keys * n_keys
    tm = _pick_tile(n_tok, (512, 384, 256, 128))
    te = 2048
    zeros = jnp.zeros((n_heads, n_keys, half), sub_keys.dtype)
    sk = jnp.concatenate([jnp.concatenate([sub_keys[:, 0], zeros], axis=2),
                          jnp.concatenate([zeros, sub_keys[:, 1]], axis=2)], axis=1).astype(BF16)
    u = peer_u.astype(BF16)
    vt = peer_v.astype(BF16).T
    row = lambda i, j: (i, 0)
    route = pltpu.VMEM((PEER_HEADS, tm // LANES, n_keys, LANES), F32)
    return pl.pallas_call(
        functools.partial(_peer_kernel, te=te, n_keys=n_keys),
        grid=(n_tok // tm, n_experts // te),
        in_specs=[pl.BlockSpec((tm, d_model), row), pl.BlockSpec((tm, d_model), row),
                  pl.BlockSpec((tm, PEER_HEADS * LANES), row),
                  pl.BlockSpec((n_heads, 2 * n_keys, LANES), lambda i, s: (0, 0, 0)),
                  pl.BlockSpec((te, d_model), lambda i, s: (s, 0)),
                  pl.BlockSpec((d_model, te), lambda i, s: (0, s))],
        out_specs=pl.BlockSpec((tm, d_model), row),
        out_shape=jax.ShapeDtypeStruct((n_tok, d_model), F32),
        scratch_shapes=[route, route, route, route, pltpu.VMEM((te, tm), F32),
                        pltpu.VMEM((te, tm), BF16), pltpu.VMEM((d_model, tm), F32)],
        compiler_params=_params(("parallel", "arbitrary")),
        name="peer",
    )(h1, hn, pq, sk, u, vt)


def kernel(x, meta_tokens, mix_norm_gain, w_in, a_q_norm_gain, a_k_norm_gain, a_lambda_q1, a_lambda_k1, a_lambda_q2, a_lambda_k2, a_subln_gain, w_branch_a, b_q_norm_gain, b_k_norm_gain, w_branch_b, w_out, ffn_norm_gain, peer_w_query, peer_sub_keys, peer_u, peer_v):
    bsz, seq, d_model = x.shape
    depth = w_in.shape[0]
    t_real = N_META_TOKENS + seq
    t_pad = ((t_real + SEQ_BLOCK - 1) // SEQ_BLOCK) * SEQ_BLOCK
    k_sel = min(TOPK_LIMIT, seq // 4)

    meta = jnp.broadcast_to(meta_tokens.astype(x.dtype)[None], (bsz, N_META_TOKENS, d_model))
    pad = jnp.zeros((bsz, t_pad - t_real, d_model), x.dtype)
    h = jnp.concatenate([meta, x, pad], axis=1).reshape(bsz * t_pad, d_model)

    q_scale = HEAD_DIM ** -0.5 * LOG2_E
    for layer in range(depth):
        lambda_init = 0.8 - 0.6 * math.exp(-0.3 * layer)
        aqg = (jnp.tile(a_q_norm_gain[layer], 8) * q_scale).reshape(1, 512)
        akg = jnp.tile(a_k_norm_gain[layer], 8).reshape(1, 512)
        bqg = (jnp.tile(b_q_norm_gain[layer], 8) * q_scale).reshape(1, 512)
        bkg = jnp.tile(b_k_norm_gain[layer], 2).reshape(1, LANES)
        p = _input_projection(h, t_pad, mix_norm_gain[layer], w_in[layer], aqg, akg, bqg, bkg)
        vec = lambda a: a[layer].astype(F32).reshape(1, HEAD_DIM)
        a_out = _diff_attention(p["aq"], p["ak"], p["av"], bsz, t_pad, vec(a_lambda_q1), vec(a_lambda_k1),
                                vec(a_lambda_q2), vec(a_lambda_k2), a_subln_gain[layer].reshape(1, LANES), lambda_init)
        b_out = _dsa_attention(p["iq"], p["iw"], p["ik"], p["bq"], p["bk"], p["bv"], bsz, t_pad, k_sel)
        h1, hn, pq = _merge(h, a_out, b_out, p["ga"], p["gb"], w_branch_a[layer], w_branch_b[layer], w_out[layer],
                            ffn_norm_gain[layer], peer_w_query[layer])
        h = _peer(h1, hn, pq, peer_sub_keys[layer], peer_u[layer], peer_v[layer])

    return h.reshape(bsz, t_pad, d_model)[:, N_META_TOKENS:N_META_TOKENS + seq]
```

```python
import functools
import math

import jax
import jax.numpy as jnp
from jax import lax
from jax.experimental import pallas as pl
from jax.experimental.pallas import tpu as pltpu

F32 = jnp.float32
BF16 = jnp.bfloat16
I32 = jnp.int32

N_META_TOKENS = 16
SEQ_BLOCK = 128
ROPE_BASE = 10000.0
NORM_EPS = 1e-6
HEAD_DIM = 64
LANES = 128
A_HEADS = 4
B_HEADS = 8
IDX_HEADS = 8
IDX_ROPE = 32
TOPK_LIMIT = 256
PEER_HEADS = 8
PEER_TOPK = 16
MASKED_SCORE = -1e30
RUNNING_MAX_FLOOR = -1e29
INT_MIN = -(2 ** 31)
LOG2_E = math.log2(math.e)
INV_SQRT2 = 1.0 / math.sqrt(2.0)
VMEM_LIMIT = 56 * 1024 * 1024


def _pick_tile(n, candidates):
    for c in candidates:
        if n % c == 0:
            return c
    raise ValueError(f"no tile in {candidates} divides {n}")


def _params(semantics):
    return pltpu.CompilerParams(dimension_semantics=semantics, vmem_limit_bytes=VMEM_LIMIT)


def _dot(a, b):
    return jnp.dot(a, b, preferred_element_type=F32)


def _dot_nt(a, b):
    return lax.dot_general(a, b, (((1,), (1,)), ((), ())), preferred_element_type=F32)


def _lane_iota(shape):
    return lax.broadcasted_iota(I32, shape, 1)


_SEC = {}
_off = 0
for _name, _w in (("aq", 512), ("ak", 512), ("av", 512), ("bq", 512), ("bk", 128), ("bv", 128),
                  ("iq", 512), ("ik", 128), ("iw", 128), ("ga", 1024), ("gb", 1024)):
    _SEC[_name] = (_off, _off + _w)
    _off += _w
IN_COLS = _off


def _head_norm(x, gain, ones_blk):
    w = x.shape[1]
    x2 = x * x
    hi = x2.astype(BF16)
    lo = (x2 - hi.astype(F32)).astype(BF16)
    m = ones_blk[:w, :w]
    ms = _dot(hi, m) + _dot(lo, m)
    return x * lax.rsqrt(ms + NORM_EPS) * gain


def _rotary(x, cos, sin_signed, half):
    w = x.shape[1]
    lane = _lane_iota(x.shape)
    fwd = pltpu.roll(x, w - half, 1)
    bwd = pltpu.roll(x, half, 1)
    partner = jnp.where((lane & half) == 0, fwd, bwd)
    return x * cos + partner * sin_signed


def _inproj_kernel(h_ref, gain_ref, w_ref, ones_ref, cos_ref, sin_ref, icos_ref, isin_ref,
                   aqg_ref, akg_ref, bqg_ref, bkg_ref,
                   aq_o, ak_o, av_o, bq_o, bk_o, bv_o, iq_o, ik_o, iw_o, ga_o, gb_o, *, iw_scale):
    h = h_ref[...]
    xn = (h * lax.rsqrt(jnp.mean(h * h, axis=-1, keepdims=True) + NORM_EPS) * gain_ref[...]).astype(BF16)

    def proj(name):
        a, b = _SEC[name]
        return _dot(xn, w_ref[:, a:b])

    ones_blk = ones_ref[...]
    cos = cos_ref[...]
    sin = sin_ref[...]
    aq_o[...] = _rotary(_head_norm(proj("aq"), aqg_ref[...], ones_blk), cos, sin, 32).astype(BF16)
    ak_o[...] = _rotary(_head_norm(proj("ak"), akg_ref[...], ones_blk), cos, sin, 32).astype(BF16)
    av_o[...] = proj("av").astype(BF16)
    bq_o[...] = _rotary(_head_norm(proj("bq"), bqg_ref[...], ones_blk), cos, sin, 32).astype(BF16)
    bk_o[...] = _rotary(_head_norm(proj("bk"), bkg_ref[...], ones_blk), cos[:, :LANES], sin[:, :LANES], 32).astype(BF16)
    bv = proj("bv")
    bv_o[...] = jnp.where(_lane_iota(bv.shape) < HEAD_DIM, bv, 1.0).astype(BF16)
    icos = icos_ref[...]
    isin = isin_ref[...]
    iq_o[...] = _rotary(proj("iq"), icos, isin, 16).astype(BF16)
    ik_o[...] = _rotary(proj("ik"), icos[:, :LANES], isin[:, :LANES], 16).astype(BF16)
    iw_o[...] = proj("iw") * iw_scale
    ga_o[...] = jax.nn.sigmoid(proj("ga")).astype(BF16)
    gb_o[...] = jax.nn.sigmoid(proj("gb")).astype(BF16)


def _rope_tables(t_len, dim, width_dim):
    inv = ROPE_BASE ** (-jnp.arange(0, dim, 2, dtype=F32) / dim)
    ang = jnp.arange(t_len, dtype=F32)[:, None] * inv[None, :]
    cos, sin = jnp.cos(ang), jnp.sin(ang)
    rest = width_dim - dim
    cos_h = jnp.concatenate([cos, cos, jnp.ones((t_len, rest), F32)], axis=1)
    sin_h = jnp.concatenate([-sin, sin, jnp.zeros((t_len, rest), F32)], axis=1)
    return cos_h, sin_h


def _input_projection(h2d, t_pad, mix_gain, w_in, aqg, akg, bqg, bkg):
    n_tok, d_model = h2d.shape
    tm = _pick_tile(t_pad, (384, 256, 128))
    per_seq = t_pad // tm

    sizes = (512, 512, 512, 512, 64, 64, 512, 64, IDX_HEADS, d_model, d_model)
    offs = [0]
    for s in sizes:
        offs.append(offs[-1] + s)
    cols = [w_in[:, offs[i]:offs[i + 1]] for i in range(len(sizes))]
    waq, wak, wav, wbq, wbk, wbv, wiq, wik, wiw, wga, wgb = cols
    wiw = jnp.pad(wiw, ((0, 0), (0, LANES - IDX_HEADS)))
    w_all = jnp.concatenate([waq, wak, wav, wbq, wbk, wbk, wbv, wbv, wiq, wik, wik, wiw, wga, wgb],
                            axis=1).astype(BF16)
    assert w_all.shape[1] == IN_COLS

    cos_h, sin_h = _rope_tables(t_pad, HEAD_DIM, HEAD_DIM)
    cos = jnp.tile(cos_h, (1, 8))
    sin = jnp.tile(sin_h, (1, 8))
    icos_h, isin_h = _rope_tables(t_pad, IDX_ROPE, HEAD_DIM)
    icos = jnp.tile(icos_h, (1, 8))
    isin = jnp.tile(isin_h, (1, 8))
    head_id = jnp.arange(512) // HEAD_DIM
    ones_blk = jnp.where(head_id[:, None] == head_id[None, :], 1.0 / HEAD_DIM, 0.0).astype(BF16)

    row = lambda i: (i, 0)
    fixed = lambda i: (0, 0)
    tab = lambda i: (i % per_seq, 0)
    widths = dict(aq=512, ak=512, av=512, bq=512, bk=128, bv=128, iq=512, ik=128, iw=128, ga=d_model, gb=d_model)
    out_dtypes = dict(iw=F32)
    names = ("aq", "ak", "av", "bq", "bk", "bv", "iq", "ik", "iw", "ga", "gb")
    out_shape = tuple(jax.ShapeDtypeStruct((n_tok, widths[k]), out_dtypes.get(k, BF16)) for k in names)
    out_specs = tuple(pl.BlockSpec((tm, widths[k]), row) for k in names)
    in_specs = [
        pl.BlockSpec((tm, d_model), row),
        pl.BlockSpec((1, d_model), fixed),
        pl.BlockSpec((d_model, IN_COLS), fixed),
        pl.BlockSpec((512, 512), fixed),
        pl.BlockSpec((tm, 512), tab), pl.BlockSpec((tm, 512), tab),
        pl.BlockSpec((tm, 512), tab), pl.BlockSpec((tm, 512), tab),
        pl.BlockSpec((1, 512), fixed), pl.BlockSpec((1, 512), fixed),
        pl.BlockSpec((1, 512), fixed), pl.BlockSpec((1, LANES), fixed),
    ]
    iw_scale = IDX_HEADS ** -0.5 * HEAD_DIM ** -0.5
    outs = pl.pallas_call(
        functools.partial(_inproj_kernel, iw_scale=iw_scale),
        grid=(n_tok // tm,),
        in_specs=in_specs, out_specs=out_specs, out_shape=out_shape,
        compiler_params=_params(("parallel",)),
        name="inproj",
    )(h2d, mix_gain.reshape(1, d_model), w_all, ones_blk, cos, sin, icos, isin, aqg, akg, bqg, bkg)
    return dict(zip(names, outs))


def _stack_heads(x):
    rows, width = x.shape
    lane = _lane_iota((rows, LANES))
    zero = jnp.zeros((rows, LANES), x.dtype)
    groups = []
    for c in range(width // LANES):
        blk = x[:, c * LANES:(c + 1) * LANES]
        groups.append(jnp.where(lane < HEAD_DIM, blk, zero))
        groups.append(jnp.where(lane >= HEAD_DIM, blk, zero))
    return jnp.concatenate(groups, axis=0)


def _diff_kernel(q_ref, k_ref, v_ref, lq1_ref, lk1_ref, lq2_ref, lk2_ref, gain_ref, o_ref,
                 qs_ref, s_ref, p_ref, m_ref, l_ref, alpha_ref, acc_ref, *, tq, tk, lambda_init):
    assert tq == tk
    i = pl.program_id(1)
    rows = 2 * tq
    lane_blocks = tk // LANES
    heads = [(h, slice(h * LANES, (h + 1) * LANES)) for h in range(A_HEADS)]
    for h, cols in heads:
        qs_ref[h] = _stack_heads(q_ref[:, cols])
    m_ref[...] = jnp.full(m_ref.shape, RUNNING_MAX_FLOOR, F32)
    l_ref[...] = jnp.zeros(l_ref.shape, F32)
    acc_ref[...] = jnp.zeros(acc_ref.shape, F32)
    blocks = [slice(r0, r0 + SEQ_BLOCK) for r0 in range(0, rows, SEQ_BLOCK)]

    def chunk(j, on_diagonal):
        start = pl.multiple_of(j * tk, tk)
        for h, cols in heads:
            s_ref[h] = _dot_nt(qs_ref[h], k_ref[pl.ds(start, tk), cols])
        for h, _ in heads:
            for blk in blocks:
                s = s_ref[h, blk, :]
                if on_diagonal:
                    qpos = blk.start % tq + lax.broadcasted_iota(I32, (SEQ_BLOCK, 1), 0)
                    s = jnp.where(_lane_iota((SEQ_BLOCK, tk)) <= qpos, s, MASKED_SCORE)
                    s_ref[h, blk, :] = s
                m_prev = m_ref[h, blk, :]
                m_new = jnp.maximum(m_prev, jnp.max(s, axis=1, keepdims=True))
                alpha_ref[h, blk, :] = jnp.exp2(m_prev - m_new)
                m_ref[h, blk, :] = m_new
        for h, cols in heads:
            for blk in blocks:
                p = jnp.exp2(s_ref[h, blk, :] - jnp.concatenate([m_ref[h, blk, :]] * lane_blocks, axis=1))
                l_ref[h, blk, :] = alpha_ref[h, blk, :] * l_ref[h, blk, :] + jnp.sum(p, axis=1, keepdims=True)
                p_ref[h, blk, :] = p.astype(BF16)
            acc_ref[h] = alpha_ref[h] * acc_ref[h] + _dot(p_ref[h], v_ref[pl.ds(start, tk), cols])

    def body(j, _):
        chunk(j, on_diagonal=False)
        return 0

    lax.fori_loop(0, i, body, 0)
    chunk(i, on_diagonal=True)
    lam = (jnp.exp(jnp.sum(lq1_ref[...] * lk1_ref[...], axis=-1, keepdims=True))
           - jnp.exp(jnp.sum(lq2_ref[...] * lk2_ref[...], axis=-1, keepdims=True)) + lambda_init)
    for h, cols in heads:
        o = acc_ref[h] / l_ref[h]
        d = o[:tq] - lam * o[tq:]
        d = d * lax.rsqrt(jnp.mean(d * d, axis=-1, keepdims=True) + NORM_EPS) * gain_ref[...]
        o_ref[:, cols] = (d * (1.0 - lambda_init)).astype(BF16)


def _diff_attention(aq, ak, av, bsz, t_pad, lq1, lk1, lq2, lk2, subln_gain, lambda_init):
    tq = _pick_tile(t_pad, (384, 256, 128))
    tk = tq
    nq = t_pad // tq
    width = A_HEADS * LANES
    qmap = lambda b, i: (b * nq + i, 0)
    kvmap = lambda b, i: (b, 0)
    fixed = lambda b, i: (0, 0)
    vec = pl.BlockSpec((1, HEAD_DIM), fixed)
    stats = pltpu.VMEM((A_HEADS, 2 * tq, LANES), F32)
    return pl.pallas_call(
        functools.partial(_diff_kernel, tq=tq, tk=tk, lambda_init=lambda_init),
        grid=(bsz, nq),
        in_specs=[pl.BlockSpec((tq, width), qmap), pl.BlockSpec((t_pad, width), kvmap),
                  pl.BlockSpec((t_pad, width), kvmap), vec, vec, vec, vec, pl.BlockSpec((1, LANES), fixed)],
        out_specs=pl.BlockSpec((tq, width), qmap),
        out_shape=jax.ShapeDtypeStruct(aq.shape, BF16),
        scratch_shapes=[pltpu.VMEM((A_HEADS, 2 * tq, LANES), BF16), pltpu.VMEM((A_HEADS, 2 * tq, tk), F32),
                        pltpu.VMEM((A_HEADS, 2 * tq, tk), BF16), stats, stats, stats, stats],
        compiler_params=_params(("parallel", "arbitrary")),
        name="diff_attn",
    )(aq, ak, av, lq1, lk1, lq2, lk2, subln_gain)


def _dsa_kernel(iq_ref, iw_ref, ik_ref, bq_ref, bk_ref, bv_ref, tri_ref, o_ref,
                key_ref, iqs_ref, qs_ref, st_ref, s_ref, p_ref, bias_ref, m_ref, alpha_ref, acc_ref, *, tq, tk, k_sel):
    i = pl.program_id(1)
    n_chunks = ((i + 1) * tq + tk - 1) // tk
    lane_blocks = tk // LANES
    qpos_row = i * tq + _lane_iota((1, tq))

    iqs_ref[...] = _stack_heads(iq_ref[...])
    iw_t = iw_ref[...].T

    def score_body(j, _):
        start = pl.multiple_of(j * tk, tk)
        st_ref[...] = _dot_nt(ik_ref[pl.ds(start, tk), :], iqs_ref[...])
        for r0 in range(0, tk, SEQ_BLOCK):
            score = jnp.zeros((SEQ_BLOCK, tq), F32)
            for h in range(IDX_HEADS):
                logits = st_ref[r0:r0 + SEQ_BLOCK, h * tq:(h + 1) * tq]
                score = score + jnp.maximum(logits, 0.0) * iw_t[h:h + 1, :]
            kpos = j * tk + r0 + lax.broadcasted_iota(I32, (SEQ_BLOCK, 1), 0)
            score = jnp.where(kpos <= qpos_row, score, -jnp.inf)
            bits = lax.bitcast_convert_type(score, I32)
            key_ref[pl.ds(start + r0, SEQ_BLOCK), :] = jnp.where(bits < 0, (bits ^ 0x7FFFFFFF) + 1, bits)
        return 0

    lax.fori_loop(0, n_chunks, score_body, 0)

    def count_ge(cand_row):
        cand = jnp.broadcast_to(cand_row, (8, tq))

        def body(j, acc):
            start = pl.multiple_of(j * tk, tk)
            for r0 in range(0, tk, 8):
                acc = acc + jnp.where(key_ref[pl.ds(start + r0, 8), :] >= cand, 1.0, 0.0)
            return acc

        acc = lax.fori_loop(0, n_chunks, body, jnp.zeros((8, tq), F32))
        return jnp.sum(acc, axis=0, keepdims=True)

    kf = float(k_sel)
    thr = jnp.where(count_ge(jnp.zeros((1, tq), I32)) >= kf, 0, INT_MIN).astype(I32)

    def bit_body(t, thr):
        cand = thr + jnp.left_shift(jnp.int32(1), 30 - t)
        return jnp.where(count_ge(cand) >= kf, cand, thr)

    thr = lax.fori_loop(0, 31, bit_body, thr)
    ties_kept = kf - count_ge(thr + 1)

    qs_ref[...] = _stack_heads(bq_ref[...])
    m_ref[...] = jnp.full(m_ref.shape, RUNNING_MAX_FLOOR, F32)
    acc_ref[...] = jnp.zeros(acc_ref.shape, F32)

    def attn_body(j, ties_before):
        start = pl.multiple_of(j * tk, tk)
        key = key_ref[pl.ds(start, tk), :]
        tie = key == thr
        tie_count = ties_before + _dot(tri_ref[...], jnp.where(tie, 1.0, 0.0).astype(BF16))
        kpos = j * tk + lax.broadcasted_iota(I32, (tk, 1), 0)
        sel = ((key > thr) | (tie & (tie_count <= ties_kept))) & (kpos <= qpos_row)
        bias_ref[...] = jnp.where(sel, 0.0, MASKED_SCORE).T
        s_ref[...] = _dot_nt(qs_ref[...], bk_ref[pl.ds(start, tk), :])
        blocks = [(slice(h * tq + r0, h * tq + r0 + SEQ_BLOCK), slice(r0, r0 + SEQ_BLOCK))
                  for h in range(B_HEADS) for r0 in range(0, tq, SEQ_BLOCK)]
        for rows, qrows in blocks:
            s = s_ref[rows, :] + bias_ref[qrows, :]
            m_prev = m_ref[rows, :]
            m_new = jnp.maximum(m_prev, jnp.max(s, axis=1, keepdims=True))
            alpha_ref[rows, :] = jnp.exp2(m_prev - m_new)
            m_ref[rows, :] = m_new
        for rows, qrows in blocks:
            m_new = jnp.concatenate([m_ref[rows, :]] * lane_blocks, axis=1)
            p_ref[rows, :] = jnp.exp2(s_ref[rows, :] + bias_ref[qrows, :] - m_new).astype(BF16)
        acc_ref[...] = alpha_ref[...] * acc_ref[...] + _dot(p_ref[...], bv_ref[pl.ds(start, tk), :])
        return tie_count[tk - 1:tk, :]

    lax.fori_loop(0, n_chunks, attn_body, jnp.zeros((1, tq), F32))
    lane = _lane_iota((tq, LANES))
    for c in range(B_HEADS // 2):
        even = acc_ref[(2 * c) * tq:(2 * c + 1) * tq, :]
        odd = acc_ref[(2 * c + 1) * tq:(2 * c + 2) * tq, :]
        o_ref[:, c * LANES:(c + 1) * LANES] = jnp.where(
            lane < HEAD_DIM, even / pltpu.roll(even, HEAD_DIM, 1), pltpu.roll(odd, HEAD_DIM, 1) / odd).astype(BF16)


def _dsa_attention(iq, iw, ik, bq, bk, bv, bsz, t_pad, k_sel):
    tq = _pick_tile(t_pad, (384, 256, 128))
    tk = tq
    nq = t_pad // tq
    qmap = lambda b, i: (b * nq + i, 0)
    kvmap = lambda b, i: (b, 0)
    tri = (jnp.arange(tk)[:, None] >= jnp.arange(tk)[None, :]).astype(BF16)
    stacked = pltpu.VMEM((B_HEADS * tq, LANES), BF16)
    stats = pltpu.VMEM((B_HEADS * tq, LANES), F32)
    return pl.pallas_call(
        functools.partial(_dsa_kernel, tq=tq, tk=tk, k_sel=k_sel),
        grid=(bsz, nq),
        in_specs=[pl.BlockSpec((tq, 512), qmap), pl.BlockSpec((tq, LANES), qmap), pl.BlockSpec((t_pad, LANES), kvmap),
                  pl.BlockSpec((tq, 512), qmap), pl.BlockSpec((t_pad, LANES), kvmap), pl.BlockSpec((t_pad, LANES), kvmap),
                  pl.BlockSpec((tk, tk), lambda b, i: (0, 0))],
        out_specs=pl.BlockSpec((tq, 512), qmap),
        out_shape=jax.ShapeDtypeStruct(bq.shape, BF16),
        scratch_shapes=[pltpu.VMEM((t_pad, tq), I32), stacked, stacked,
                        pltpu.VMEM((tk, B_HEADS * tq), F32), pltpu.VMEM((B_HEADS * tq, tk), F32), pltpu.VMEM((B_HEADS * tq, tk), BF16),
                        pltpu.VMEM((tq, tk), F32), stats, stats, stats],
        compiler_params=_params(("parallel", "arbitrary")),
        name="dsa_attn",
    )(iq, iw, ik, bq, bk, bv, tri)


def _merge_kernel(h_ref, a_ref, b_ref, ga_ref, gb_ref, wa_ref, wb_ref, wo_ref, gain_ref, wq_ref,
                  h1_o, hn_o, pq_o):
    ya = _dot(a_ref[...], wa_ref[...])
    yb = _dot(b_ref[...], wb_ref[...])
    mixed = ga_ref[...].astype(F32) * ya + gb_ref[...].astype(F32) * yb
    h1 = h_ref[...] + _dot(mixed.astype(BF16), wo_ref[...])
    h1_o[...] = h1
    hn = h1 * lax.rsqrt(jnp.mean(h1 * h1, axis=-1, keepdims=True) + NORM_EPS) * gain_ref[...]
    hn_o[...] = (hn * INV_SQRT2).astype(BF16)
    pq_o[...] = _dot(hn.astype(BF16), wq_ref[...]).astype(BF16)


def _merge(h2d, a_out, b_out, ga, gb, w_a, w_b, w_o, ffn_gain, w_query):
    n_tok, d_model = h2d.shape
    tm = _pick_tile(n_tok, (256, 128))
    row = lambda i: (i, 0)
    fixed = lambda i: (0, 0)
    pw = w_query.shape[1]
    return pl.pallas_call(
        _merge_kernel,
        grid=(n_tok // tm,),
        in_specs=[pl.BlockSpec((tm, d_model), row), pl.BlockSpec((tm, 512), row), pl.BlockSpec((tm, 512), row),
                  pl.BlockSpec((tm, d_model), row), pl.BlockSpec((tm, d_model), row),
                  pl.BlockSpec((512, d_model), fixed), pl.BlockSpec((512, d_model), fixed),
                  pl.BlockSpec((d_model, d_model), fixed), pl.BlockSpec((1, d_model), fixed),
                  pl.BlockSpec((d_model, pw), fixed)],
        out_specs=(pl.BlockSpec((tm, d_model), row), pl.BlockSpec((tm, d_model), row), pl.BlockSpec((tm, pw), row)),
        out_shape=(jax.ShapeDtypeStruct((n_tok, d_model), F32), jax.ShapeDtypeStruct((n_tok, d_model), BF16),
                   jax.ShapeDtypeStruct((n_tok, pw), BF16)),
        compiler_params=_params(("parallel",)),
        name="merge",
    )(h2d, a_out, b_out, ga, gb, w_a.astype(BF16), w_b.astype(BF16), w_o.astype(BF16),
      ffn_gain.reshape(1, d_model), w_query.astype(BF16))


def _top_values(xs, count):
    curs = list(xs)
    ranks = [jnp.full(x.shape, float(count), F32) for x in xs]
    vals = [[] for _ in xs]
    for it in range(count):
        for k in range(len(curs)):
            m = jnp.max(curs[k], axis=0, keepdims=True)
            vals[k].append(m)
            hit = curs[k] == m
            ranks[k] = jnp.where(hit, float(it), ranks[k])
            curs[k] = jnp.where(hit, -jnp.inf, curs[k])
    return [(jnp.concatenate(v, axis=0), r) for v, r in zip(vals, ranks)]


def _pair_sum_candidates(v1, v2):
    k = v1.shape[0]
    pieces = []
    for a in range(k // 2):
        nb = min(k, -(-(k // (a + 1)) // 8) * 8)
        pieces.append(v1[a:a + 1] + v2[:nb])
    pieces.append(v1[k // 2:] + v2[0:1])
    return jnp.concatenate(pieces, axis=0)


def _peer_routing(pq_ref, sk_ref, cut_ref, e1_ref, rank_ref, e2_ref, n_keys):
    def head_body(h, _):
        lo = pl.multiple_of(h * LANES, LANES)
        st = _dot_nt(sk_ref[h], pq_ref[:, pl.ds(lo, LANES)])
        n_blocks = st.shape[1] // LANES
        s1 = [st[:n_keys, c * LANES:(c + 1) * LANES] for c in range(n_blocks)]
        s2 = [st[n_keys:, c * LANES:(c + 1) * LANES] for c in range(n_blocks)]
        tops = _top_values(s1 + s2, PEER_TOPK)
        v1 = [t[0] for t in tops[:n_blocks]]
        v2 = [t[0] for t in tops[n_blocks:]]
        rank = [t[1] for t in tops[n_blocks:]]
        cand = [_pair_sum_candidates(a, b) for a, b in zip(v1, v2)]
        thr = [t[0][PEER_TOPK - 1:PEER_TOPK] for t in _top_values(cand, PEER_TOPK)]
        for c in range(n_blocks):
            top = v1[c][0:1] + v2[c][0:1]
            z = jnp.sum(jnp.where(cand[c] >= thr[c], jnp.exp(cand[c] - top), 0.0), axis=0, keepdims=True)
            cut = jnp.zeros_like(s1[c])
            for b in range(PEER_TOPK):
                cut = cut + jnp.where(s1[c] + v2[c][b:b + 1] >= thr[c], 1.0, 0.0)
            cut_ref[h, c] = cut
            e1_ref[h, c] = jnp.exp(s1[c] - v1[c][0:1]) * INV_SQRT2
            rank_ref[h, c] = rank[c]
            e2_ref[h, c] = jnp.exp(s2[c] - v2[c][0:1]) / z
        return 0

    lax.fori_loop(0, PEER_HEADS, head_body, 0, unroll=2)


def _gelu_scaled(y):
    return y * (1.0 + lax.erf(y))


def _peer_kernel(h1_ref, hn_ref, pq_ref, sk_ref, u_ref, vt_ref, o_ref,
                 cut_ref, e1_ref, rank_ref, e2_ref, act_ref, p_ref, acc_ref, *, te, n_keys):
    s = pl.program_id(1)
    tm = hn_ref.shape[0]
    rows_per_tile = te // n_keys

    @pl.when(s == 0)
    def _():
        _peer_routing(pq_ref, sk_ref, cut_ref, e1_ref, rank_ref, e2_ref, n_keys)
        acc_ref[...] = jnp.zeros_like(acc_ref)

    act_ref[...] = _dot_nt(u_ref[...], hn_ref[...])

    def chunk_body(c, _):
        cols = pl.ds(pl.multiple_of(c * LANES, LANES), LANES)
        for row8 in range(rows_per_tile // 8):
            rows = pl.ds(pl.multiple_of(s * rows_per_tile + row8 * 8, 8), 8)
            cut8 = [cut_ref[h, c, rows, :] for h in range(PEER_HEADS)]
            e18 = [e1_ref[h, c, rows, :] for h in range(PEER_HEADS)]
            for k in range(8):
                row = row8 * 8 + k
                w = jnp.zeros((n_keys, LANES), F32)
                for h in range(PEER_HEADS):
                    cut_b = jnp.broadcast_to(cut8[h][k:k + 1], (n_keys, LANES))
                    e1_b = jnp.broadcast_to(e18[h][k:k + 1], (n_keys, LANES))
                    w = w + jnp.where(rank_ref[h, c] < cut_b, e2_ref[h, c], 0.0) * e1_b
                a = act_ref[row * n_keys:(row + 1) * n_keys, cols]
                p_ref[row * n_keys:(row + 1) * n_keys, cols] = (w * _gelu_scaled(a)).astype(BF16)
        return 0

    lax.fori_loop(0, tm // LANES, chunk_body, 0)
    acc_ref[...] += _dot(vt_ref[...], p_ref[...])

    @pl.when(s == pl.num_programs(1) - 1)
    def _():
        o_ref[...] = h1_ref[...] + acc_ref[...].T


def _peer(h1, hn, pq, sub_keys, peer_u, peer_v):
    n_tok, d_model = h1.shape
    n_heads, _, n_keys, half = sub_keys.shape
    n_experts = peer_u.shape[0]
    assert n_heads == PEER_HEADS and n_keys == LANES and 2 * half == LANES and n_experts == n_keys * n_keys
    tm = _pick_tile(n_tok, (512, 384, 256, 128))
    te = 2048
    zeros = jnp.zeros((n_heads, n_keys, half), sub_keys.dtype)
    sk = jnp.concatenate([jnp.concatenate([sub_keys[:, 0], zeros], axis=2),
                          jnp.concatenate([zeros, sub_keys[:, 1]], axis=2)], axis=1).astype(BF16)
    u = peer_u.astype(BF16)
    vt = peer_v.astype(BF16).T
    row = lambda i, j: (i, 0)
    route = pltpu.VMEM((PEER_HEADS, tm // LANES, n_keys, LANES), F32)
    return pl.pallas_call(
        functools.partial(_peer_kernel, te=te, n_keys=n_keys),
        grid=(n_tok // tm, n_experts // te),
        in_specs=[pl.BlockSpec((tm, d_model), row), pl.BlockSpec((tm, d_model), row),
                  pl.BlockSpec((tm, PEER_HEADS * LANES), row),
                  pl.BlockSpec((n_heads, 2 * n_keys, LANES), lambda i, s: (0, 0, 0)),
                  pl.BlockSpec((te, d_model), lambda i, s: (s, 0)),
                  pl.BlockSpec((d_model, te), lambda i, s: (0, s))],
        out_specs=pl.BlockSpec((tm, d_model), row),
        out_shape=jax.ShapeDtypeStruct((n_tok, d_model), F32),
        scratch_shapes=[route, route, route, route, pltpu.VMEM((te, tm), F32),
                        pltpu.VMEM((te, tm), BF16), pltpu.VMEM((d_model, tm), F32)],
        compiler_params=_params(("parallel", "arbitrary")),
        name="peer",
    )(h1, hn, pq, sk, u, vt)


def kernel(x, meta_tokens, mix_norm_gain, w_in, a_q_norm_gain, a_k_norm_gain, a_lambda_q1, a_lambda_k1, a_lambda_q2, a_lambda_k2, a_subln_gain, w_branch_a, b_q_norm_gain, b_k_norm_gain, w_branch_b, w_out, ffn_norm_gain, peer_w_query, peer_sub_keys, peer_u, peer_v):
    bsz, seq, d_model = x.shape
    depth = w_in.shape[0]
    t_real = N_META_TOKENS + seq
    t_pad = ((t_real + SEQ_BLOCK - 1) // SEQ_BLOCK) * SEQ_BLOCK
    k_sel = min(TOPK_LIMIT, seq // 4)

    meta = jnp.broadcast_to(meta_tokens.astype(x.dtype)[None], (bsz, N_META_TOKENS, d_model))
    pad = jnp.zeros((bsz, t_pad - t_real, d_model), x.dtype)
    h = jnp.concatenate([meta, x, pad], axis=1).reshape(bsz * t_pad, d_model)

    q_scale = HEAD_DIM ** -0.5 * LOG2_E
    for layer in range(depth):
        lambda_init = 0.8 - 0.6 * math.exp(-0.3 * layer)
        aqg = (jnp.tile(a_q_norm_gain[layer], 8) * q_scale).reshape(1, 512)
        akg = jnp.tile(a_k_norm_gain[layer], 8).reshape(1, 512)
        bqg = (jnp.tile(b_q_norm_gain[layer], 8) * q_scale).reshape(1, 512)
        bkg = jnp.tile(b_k_norm_gain[layer], 2).reshape(1, LANES)
        p = _input_projection(h, t_pad, mix_norm_gain[layer], w_in[layer], aqg, akg, bqg, bkg)
        vec = lambda a: a[layer].astype(F32).reshape(1, HEAD_DIM)
        a_out = _diff_attention(p["aq"], p["ak"], p["av"], bsz, t_pad, vec(a_lambda_q1), vec(a_lambda_k1),
                                vec(a_lambda_q2), vec(a_lambda_k2), a_subln_gain[layer].reshape(1, LANES), lambda_init)
        b_out = _dsa_attention(p["iq"], p["iw"], p["ik"], p["bq"], p["bk"], p["bv"], bsz, t_pad, k_sel)
        h1, hn, pq = _merge(h, a_out, b_out, p["ga"], p["gb"], w_branch_a[layer], w_branch_b[layer], w_out[layer],
                            ffn_norm_gain[layer], peer_w_query[layer])
        h = _peer(h1, hn, pq, peer_sub_keys[layer], peer_u[layer], peer_v[layer])

    return h.reshape(bsz, t_pad, d_model)[:, N_META_TOKENS:N_META_TOKENS + seq]
```

```python
import functools
import math

import jax
import jax.numpy as jnp
from jax import lax
from jax.experimental import pallas as pl
from jax.experimental.pallas import tpu as pltpu

F32 = jnp.float32
BF16 = jnp.bfloat16
I32 = jnp.int32

N_META_TOKENS = 16
SEQ_BLOCK = 128
ROPE_BASE = 10000.0
NORM_EPS = 1e-6
HEAD_DIM = 64
LANES = 128
A_HEADS = 4
B_HEADS = 8
IDX_HEADS = 8
IDX_ROPE = 32
TOPK_LIMIT = 256
PEER_HEADS = 8
PEER_TOPK = 16
MASKED_SCORE = -1e30
RUNNING_MAX_FLOOR = -1e29
INT_MIN = -(2 ** 31)
LOG2_E = math.log2(math.e)
INV_SQRT2 = 1.0 / math.sqrt(2.0)
VMEM_LIMIT = 56 * 1024 * 1024


def _pick_tile(n, candidates):
    for c in candidates:
        if n % c == 0:
            return c
    raise ValueError(f"no tile in {candidates} divides {n}")


def _params(semantics):
    return pltpu.CompilerParams(dimension_semantics=semantics, vmem_limit_bytes=VMEM_LIMIT)


def _dot(a, b):
    return jnp.dot(a, b, preferred_element_type=F32)


def _dot_nt(a, b):
    return lax.dot_general(a, b, (((1,), (1,)), ((), ())), preferred_element_type=F32)


def _lane_iota(shape):
    return lax.broadcasted_iota(I32, shape, 1)


_SEC = {}
_off = 0
for _name, _w in (("aq", 512), ("ak", 512), ("av", 512), ("bq", 512), ("bk", 128), ("bv", 128),
                  ("iq", 512), ("ik", 128), ("iw", 128), ("ga", 1024), ("gb", 1024)):
    _SEC[_name] = (_off, _off + _w)
    _off += _w
IN_COLS = _off


def _head_norm(x, gain, ones_blk):
    w = x.shape[1]
    x2 = x * x
    hi = x2.astype(BF16)
    lo = (x2 - hi.astype(F32)).astype(BF16)
    m = ones_blk[:w, :w]
    ms = _dot(hi, m) + _dot(lo, m)
    return x * lax.rsqrt(ms + NORM_EPS) * gain


def _rotary(x, cos, sin_signed, half):
    w = x.shape[1]
    lane = _lane_iota(x.shape)
    fwd = pltpu.roll(x, w - half, 1)
    bwd = pltpu.roll(x, half, 1)
    partner = jnp.where((lane & half) == 0, fwd, bwd)
    return x * cos + partner * sin_signed


def _inproj_kernel(h_ref, gain_ref, w_ref, ones_ref, cos_ref, sin_ref, icos_ref, isin_ref,
                   aqg_ref, akg_ref, bqg_ref, bkg_ref,
                   aq_o, ak_o, av_o, bq_o, bk_o, bv_o, iq_o, ik_o, iw_o, ga_o, gb_o, *, iw_scale):
    h = h_ref[...]
    xn = (h * lax.rsqrt(jnp.mean(h * h, axis=-1, keepdims=True) + NORM_EPS) * gain_ref[...]).astype(BF16)

    def proj(name):
        a, b = _SEC[name]
        return _dot(xn, w_ref[:, a:b])

    ones_blk = ones_ref[...]
    cos1 = cos_ref[...]
    sin1 = sin_ref[...]
    cos = jnp.concatenate([cos1] * 4, axis=1)
    sin = jnp.concatenate([sin1] * 4, axis=1)
    aq_o[...] = _rotary(_head_norm(proj("aq"), aqg_ref[...], ones_blk), cos, sin, 32).astype(BF16)
    ak_o[...] = _rotary(_head_norm(proj("ak"), akg_ref[...], ones_blk), cos, sin, 32).astype(BF16)
    av_o[...] = proj("av").astype(BF16)
    bq_o[...] = _rotary(_head_norm(proj("bq"), bqg_ref[...], ones_blk), cos, sin, 32).astype(BF16)
    bk_o[...] = _rotary(_head_norm(proj("bk"), bkg_ref[...], ones_blk), cos1, sin1, 32).astype(BF16)
    bv = proj("bv")
    bv_o[...] = jnp.where(_lane_iota(bv.shape) < HEAD_DIM, bv, 1.0).astype(BF16)
    icos1 = icos_ref[...]
    isin1 = isin_ref[...]
    iq_o[...] = _rotary(proj("iq"), jnp.concatenate([icos1] * 4, axis=1), jnp.concatenate([isin1] * 4, axis=1),
                        16).astype(BF16)
    ik_o[...] = _rotary(proj("ik"), icos1, isin1, 16).astype(BF16)
    iw_o[...] = proj("iw") * iw_scale
    ga_o[...] = jax.nn.sigmoid(proj("ga")).astype(BF16)
    gb_o[...] = jax.nn.sigmoid(proj("gb")).astype(BF16)


def _rope_tables(t_len, dim, width_dim):
    inv = ROPE_BASE ** (-jnp.arange(0, dim, 2, dtype=F32) / dim)
    ang = jnp.arange(t_len, dtype=F32)[:, None] * inv[None, :]
    cos, sin = jnp.cos(ang), jnp.sin(ang)
    rest = width_dim - dim
    cos_h = jnp.concatenate([cos, cos, jnp.ones((t_len, rest), F32)], axis=1)
    sin_h = jnp.concatenate([-sin, sin, jnp.zeros((t_len, rest), F32)], axis=1)
    return cos_h, sin_h


def _input_projection(h2d, t_pad, mix_gain, w_in, aqg, akg, bqg, bkg):
    n_tok, d_model = h2d.shape
    tm = _pick_tile(t_pad, (384, 256, 128))
    per_seq = t_pad // tm

    sizes = (512, 512, 512, 512, 64, 64, 512, 64, IDX_HEADS, d_model, d_model)
    offs = [0]
    for s in sizes:
        offs.append(offs[-1] + s)
    cols = [w_in[:, offs[i]:offs[i + 1]] for i in range(len(sizes))]
    waq, wak, wav, wbq, wbk, wbv, wiq, wik, wiw, wga, wgb = cols
    wiw = jnp.pad(wiw, ((0, 0), (0, LANES - IDX_HEADS)))
    w_all = jnp.concatenate([waq, wak, wav, wbq, wbk, wbk, wbv, wbv, wiq, wik, wik, wiw, wga, wgb],
                            axis=1).astype(BF16)
    assert w_all.shape[1] == IN_COLS

    cos_h, sin_h = _rope_tables(t_pad, HEAD_DIM, HEAD_DIM)
    cos = jnp.tile(cos_h, (1, 2))
    sin = jnp.tile(sin_h, (1, 2))
    icos_h, isin_h = _rope_tables(t_pad, IDX_ROPE, HEAD_DIM)
    icos = jnp.tile(icos_h, (1, 2))
    isin = jnp.tile(isin_h, (1, 2))
    head_id = jnp.arange(512) // HEAD_DIM
    ones_blk = jnp.where(head_id[:, None] == head_id[None, :], 1.0 / HEAD_DIM, 0.0).astype(BF16)

    row = lambda i: (i, 0)
    fixed = lambda i: (0, 0)
    tab = lambda i: (i % per_seq, 0)
    widths = dict(aq=512, ak=512, av=512, bq=512, bk=128, bv=128, iq=512, ik=128, iw=128, ga=d_model, gb=d_model)
    out_dtypes = dict(iw=F32)
    names = ("aq", "ak", "av", "bq", "bk", "bv", "iq", "ik", "iw", "ga", "gb")
    out_shape = tuple(jax.ShapeDtypeStruct((n_tok, widths[k]), out_dtypes.get(k, BF16)) for k in names)
    out_specs = tuple(pl.BlockSpec((tm, widths[k]), row) for k in names)
    in_specs = [
        pl.BlockSpec((tm, d_model), row),
        pl.BlockSpec((1, d_model), fixed),
        pl.BlockSpec((d_model, IN_COLS), fixed),
        pl.BlockSpec((512, 512), fixed),
        pl.BlockSpec((tm, LANES), tab), pl.BlockSpec((tm, LANES), tab),
        pl.BlockSpec((tm, LANES), tab), pl.BlockSpec((tm, LANES), tab),
        pl.BlockSpec((1, 512), fixed), pl.BlockSpec((1, 512), fixed),
        pl.BlockSpec((1, 512), fixed), pl.BlockSpec((1, LANES), fixed),
    ]
    iw_scale = IDX_HEADS ** -0.5 * HEAD_DIM ** -0.5
    outs = pl.pallas_call(
        functools.partial(_inproj_kernel, iw_scale=iw_scale),
        grid=(n_tok // tm,),
        in_specs=in_specs, out_specs=out_specs, out_shape=out_shape,
        compiler_params=_params(("parallel",)),
        name="inproj",
    )(h2d, mix_gain.reshape(1, d_model), w_all, ones_blk, cos, sin, icos, isin, aqg, akg, bqg, bkg)
    return dict(zip(names, outs))


def _stack_heads(x):
    rows, width = x.shape
    lane = _lane_iota((rows, LANES))
    zero = jnp.zeros((rows, LANES), x.dtype)
    groups = []
    for c in range(width // LANES):
        blk = x[:, c * LANES:(c + 1) * LANES]
        groups.append(jnp.where(lane < HEAD_DIM, blk, zero))
        groups.append(jnp.where(lane >= HEAD_DIM, blk, zero))
    return jnp.concatenate(groups, axis=0)


def _diff_kernel(q_ref, k_ref, v_ref, lq1_ref, lk1_ref, lq2_ref, lk2_ref, gain_ref, o_ref,
                 qs_ref, s_ref, p_ref, m_ref, l_ref, alpha_ref, acc_ref, *, tq, tk, lambda_init):
    assert tq == tk
    i = pl.program_id(1)
    rows = 2 * tq
    lane_blocks = tk // LANES
    heads = [(h, slice(h * LANES, (h + 1) * LANES)) for h in range(A_HEADS)]
    for h, cols in heads:
        qs_ref[h] = _stack_heads(q_ref[:, cols])
    m_ref[...] = jnp.full(m_ref.shape, RUNNING_MAX_FLOOR, F32)
    l_ref[...] = jnp.zeros(l_ref.shape, F32)
    acc_ref[...] = jnp.zeros(acc_ref.shape, F32)
    blocks = [slice(r0, r0 + SEQ_BLOCK) for r0 in range(0, rows, SEQ_BLOCK)]

    def chunk(j, on_diagonal):
        start = pl.multiple_of(j * tk, tk)
        for h, cols in heads:
            s_ref[h] = _dot_nt(qs_ref[h], k_ref[pl.ds(start, tk), cols])
        for h, _ in heads:
            for blk in blocks:
                s = s_ref[h, blk, :]
                if on_diagonal:
                    qpos = blk.start % tq + lax.broadcasted_iota(I32, (SEQ_BLOCK, 1), 0)
                    s = jnp.where(_lane_iota((SEQ_BLOCK, tk)) <= qpos, s, MASKED_SCORE)
                    s_ref[h, blk, :] = s
                m_prev = m_ref[h, blk, :]
                m_new = jnp.maximum(m_prev, jnp.max(s, axis=1, keepdims=True))
                alpha_ref[h, blk, :] = jnp.exp2(m_prev - m_new)
                m_ref[h, blk, :] = m_new
        for h, cols in heads:
            for blk in blocks:
                p = jnp.exp2(s_ref[h, blk, :] - jnp.concatenate([m_ref[h, blk, :]] * lane_blocks, axis=1))
                l_ref[h, blk, :] = alpha_ref[h, blk, :] * l_ref[h, blk, :] + jnp.sum(p, axis=1, keepdims=True)
                p_ref[h, blk, :] = p.astype(BF16)
            acc_ref[h] = alpha_ref[h] * acc_ref[h] + _dot(p_ref[h], v_ref[pl.ds(start, tk), cols])

    def body(j, _):
        chunk(j, on_diagonal=False)
        return 0

    lax.fori_loop(0, i, body, 0)
    chunk(i, on_diagonal=True)
    lam = (jnp.exp(jnp.sum(lq1_ref[...] * lk1_ref[...], axis=-1, keepdims=True))
           - jnp.exp(jnp.sum(lq2_ref[...] * lk2_ref[...], axis=-1, keepdims=True)) + lambda_init)
    for h, cols in heads:
        o = acc_ref[h] / l_ref[h]
        d = o[:tq] - lam * o[tq:]
        d = d * lax.rsqrt(jnp.mean(d * d, axis=-1, keepdims=True) + NORM_EPS) * gain_ref[...]
        o_ref[:, cols] = (d * (1.0 - lambda_init)).astype(BF16)


def _diff_attention(aq, ak, av, bsz, t_pad, lq1, lk1, lq2, lk2, subln_gain, lambda_init):
    tq = _pick_tile(t_pad, (384, 256, 128))
    tk = tq
    nq = t_pad // tq
    width = A_HEADS * LANES
    qmap = lambda b, i: (b * nq + i, 0)
    kvmap = lambda b, i: (b, 0)
    fixed = lambda b, i: (0, 0)
    vec = pl.BlockSpec((1, HEAD_DIM), fixed)
    stats = pltpu.VMEM((A_HEADS, 2 * tq, LANES), F32)
    return pl.pallas_call(
        functools.partial(_diff_kernel, tq=tq, tk=tk, lambda_init=lambda_init),
        grid=(bsz, nq),
        in_specs=[pl.BlockSpec((tq, width), qmap), pl.BlockSpec((t_pad, width), kvmap),
                  pl.BlockSpec((t_pad, width), kvmap), vec, vec, vec, vec, pl.BlockSpec((1, LANES), fixed)],
        out_specs=pl.BlockSpec((tq, width), qmap),
        out_shape=jax.ShapeDtypeStruct(aq.shape, BF16),
        scratch_shapes=[pltpu.VMEM((A_HEADS, 2 * tq, LANES), BF16), pltpu.VMEM((A_HEADS, 2 * tq, tk), F32),
                        pltpu.VMEM((A_HEADS, 2 * tq, tk), BF16), stats, stats, stats, stats],
        compiler_params=_params(("parallel", "arbitrary")),
        name="diff_attn",
    )(aq, ak, av, lq1, lk1, lq2, lk2, subln_gain)


def _dsa_kernel(iq_ref, iw_ref, ik_ref, bq_ref, bk_ref, bv_ref, tri_ref, o_ref,
                key_ref, iqs_ref, qs_ref, st_ref, s_ref, p_ref, bias_ref, m_ref, alpha_ref, acc_ref, *, tq, tk, k_sel):
    i = pl.program_id(1)
    n_chunks = ((i + 1) * tq + tk - 1) // tk
    lane_blocks = tk // LANES
    qpos_row = i * tq + _lane_iota((1, tq))

    iqs_ref[...] = _stack_heads(iq_ref[...])
    iw_t = iw_ref[...].T

    def score_body(j, _):
        start = pl.multiple_of(j * tk, tk)
        st_ref[...] = _dot_nt(ik_ref[pl.ds(start, tk), :], iqs_ref[...])
        for r0 in range(0, tk, SEQ_BLOCK):
            score = jnp.zeros((SEQ_BLOCK, tq), F32)
            for h in range(IDX_HEADS):
                logits = st_ref[r0:r0 + SEQ_BLOCK, h * tq:(h + 1) * tq]
                score = score + jnp.maximum(logits, 0.0) * iw_t[h:h + 1, :]
            kpos = j * tk + r0 + lax.broadcasted_iota(I32, (SEQ_BLOCK, 1), 0)
            score = jnp.where(kpos <= qpos_row, score, -jnp.inf)
            bits = lax.bitcast_convert_type(score, I32)
            key_ref[pl.ds(start + r0, SEQ_BLOCK), :] = jnp.where(bits < 0, (bits ^ 0x7FFFFFFF) + 1, bits)
        return 0

    lax.fori_loop(0, n_chunks, score_body, 0)

    def count_ge(cand_row):
        cand = jnp.broadcast_to(cand_row, (8, tq))

        def body(j, acc):
            start = pl.multiple_of(j * tk, tk)
            for r0 in range(0, tk, 8):
                acc = acc + jnp.where(key_ref[pl.ds(start + r0, 8), :] >= cand, 1.0, 0.0)
            return acc

        acc = lax.fori_loop(0, n_chunks, body, jnp.zeros((8, tq), F32))
        return jnp.sum(acc, axis=0, keepdims=True)

    kf = float(k_sel)
    thr = jnp.where(count_ge(jnp.zeros((1, tq), I32)) >= kf, 0, INT_MIN).astype(I32)

    def bit_body(t, thr):
        cand = thr + jnp.left_shift(jnp.int32(1), 30 - t)
        return jnp.where(count_ge(cand) >= kf, cand, thr)

    thr = lax.fori_loop(0, 31, bit_body, thr)
    ties_kept = kf - count_ge(thr + 1)

    qs_ref[...] = _stack_heads(bq_ref[...])
    m_ref[...] = jnp.full(m_ref.shape, RUNNING_MAX_FLOOR, F32)
    acc_ref[...] = jnp.zeros(acc_ref.shape, F32)

    def attn_body(j, ties_before):
        start = pl.multiple_of(j * tk, tk)
        key = key_ref[pl.ds(start, tk), :]
        tie = key == thr
        tie_count = ties_before + _dot(tri_ref[...], jnp.where(tie, 1.0, 0.0).astype(BF16))
        kpos = j * tk + lax.broadcasted_iota(I32, (tk, 1), 0)
        sel = ((key > thr) | (tie & (tie_count <= ties_kept))) & (kpos <= qpos_row)
        bias_ref[...] = jnp.where(sel, 0.0, MASKED_SCORE).T
        s_ref[...] = _dot_nt(qs_ref[...], bk_ref[pl.ds(start, tk), :])
        blocks = [(slice(h * tq + r0, h * tq + r0 + SEQ_BLOCK), slice(r0, r0 + SEQ_BLOCK))
                  for h in range(B_HEADS) for r0 in range(0, tq, SEQ_BLOCK)]
        for rows, qrows in blocks:
            s = s_ref[rows, :] + bias_ref[qrows, :]
            m_prev = m_ref[rows, :]
            m_new = jnp.maximum(m_prev, jnp.max(s, axis=1, keepdims=True))
            alpha_ref[rows, :] = jnp.exp2(m_prev - m_new)
            m_ref[rows, :] = m_new
        for rows, qrows in blocks:
            m_new = jnp.concatenate([m_ref[rows, :]] * lane_blocks, axis=1)
            p_ref[rows, :] = jnp.exp2(s_ref[rows, :] + bias_ref[qrows, :] - m_new).astype(BF16)
        acc_ref[...] = alpha_ref[...] * acc_ref[...] + _dot(p_ref[...], bv_ref[pl.ds(start, tk), :])
        return tie_count[tk - 1:tk, :]

    lax.fori_loop(0, n_chunks, attn_body, jnp.zeros((1, tq), F32))
    lane = _lane_iota((tq, LANES))
    for c in range(B_HEADS // 2):
        even = acc_ref[(2 * c) * tq:(2 * c + 1) * tq, :]
        odd = acc_ref[(2 * c + 1) * tq:(2 * c + 2) * tq, :]
        o_ref[:, c * LANES:(c + 1) * LANES] = jnp.where(
            lane < HEAD_DIM, even / pltpu.roll(even, HEAD_DIM, 1), pltpu.roll(odd, HEAD_DIM, 1) / odd).astype(BF16)


def _dsa_attention(iq, iw, ik, bq, bk, bv, bsz, t_pad, k_sel):
    tq = _pick_tile(t_pad, (384, 256, 128))
    tk = tq
    nq = t_pad // tq
    qmap = lambda b, i: (b * nq + i, 0)
    kvmap = lambda b, i: (b, 0)
    tri = (jnp.arange(tk)[:, None] >= jnp.arange(tk)[None, :]).astype(BF16)
    stacked = pltpu.VMEM((B_HEADS * tq, LANES), BF16)
    stats = pltpu.VMEM((B_HEADS * tq, LANES), F32)
    return pl.pallas_call(
        functools.partial(_dsa_kernel, tq=tq, tk=tk, k_sel=k_sel),
        grid=(bsz, nq),
        in_specs=[pl.BlockSpec((tq, 512), qmap), pl.BlockSpec((tq, LANES), qmap), pl.BlockSpec((t_pad, LANES), kvmap),
                  pl.BlockSpec((tq, 512), qmap), pl.BlockSpec((t_pad, LANES), kvmap), pl.BlockSpec((t_pad, LANES), kvmap),
                  pl.BlockSpec((tk, tk), lambda b, i: (0, 0))],
        out_specs=pl.BlockSpec((tq, 512), qmap),
        out_shape=jax.ShapeDtypeStruct(bq.shape, BF16),
        scratch_shapes=[pltpu.VMEM((t_pad, tq), I32), stacked, stacked,
                        pltpu.VMEM((tk, B_HEADS * tq), F32), pltpu.VMEM((B_HEADS * tq, tk), F32), pltpu.VMEM((B_HEADS * tq, tk), BF16),
                        pltpu.VMEM((tq, tk), F32), stats, stats, stats],
        compiler_params=_params(("parallel", "arbitrary")),
        name="dsa_attn",
    )(iq, iw, ik, bq, bk, bv, tri)


def _merge_kernel(h_ref, a_ref, b_ref, ga_ref, gb_ref, wa_ref, wb_ref, wo_ref, gain_ref, wq_ref,
                  h1_o, hn_o, pq_o):
    ya = _dot(a_ref[...], wa_ref[...])
    yb = _dot(b_ref[...], wb_ref[...])
    mixed = ga_ref[...].astype(F32) * ya + gb_ref[...].astype(F32) * yb
    h1 = h_ref[...] + _dot(mixed.astype(BF16), wo_ref[...])
    h1_o[...] = h1
    hn = h1 * lax.rsqrt(jnp.mean(h1 * h1, axis=-1, keepdims=True) + NORM_EPS) * gain_ref[...]
    hn_o[...] = (hn * INV_SQRT2).astype(BF16)
    pq_o[...] = _dot(hn.astype(BF16), wq_ref[...]).astype(BF16)


def _merge(h2d, a_out, b_out, ga, gb, w_a, w_b, w_o, ffn_gain, w_query):
    n_tok, d_model = h2d.shape
    tm = _pick_tile(n_tok, (256, 128))
    row = lambda i: (i, 0)
    fixed = lambda i: (0, 0)
    pw = w_query.shape[1]
    return pl.pallas_call(
        _merge_kernel,
        grid=(n_tok // tm,),
        in_specs=[pl.BlockSpec((tm, d_model), row), pl.BlockSpec((tm, 512), row), pl.BlockSpec((tm, 512), row),
                  pl.BlockSpec((tm, d_model), row), pl.BlockSpec((tm, d_model), row),
                  pl.BlockSpec((512, d_model), fixed), pl.BlockSpec((512, d_model), fixed),
                  pl.BlockSpec((d_model, d_model), fixed), pl.BlockSpec((1, d_model), fixed),
                  pl.BlockSpec((d_model, pw), fixed)],
        out_specs=(pl.BlockSpec((tm, d_model), row), pl.BlockSpec((tm, d_model), row), pl.BlockSpec((tm, pw), row)),
        out_shape=(jax.ShapeDtypeStruct((n_tok, d_model), F32), jax.ShapeDtypeStruct((n_tok, d_model), BF16),
                   jax.ShapeDtypeStruct((n_tok, pw), BF16)),
        compiler_params=_params(("parallel",)),
        name="merge",
    )(h2d, a_out, b_out, ga, gb, w_a.astype(BF16), w_b.astype(BF16), w_o.astype(BF16),
      ffn_gain.reshape(1, d_model), w_query.astype(BF16))


def _top_values(xs, count):
    curs = list(xs)
    ranks = [jnp.full(x.shape, float(count), F32) for x in xs]
    vals = [[] for _ in xs]
    for it in range(count):
        for k in range(len(curs)):
            m = jnp.max(curs[k], axis=0, keepdims=True)
            vals[k].append(m)
            hit = curs[k] == m
            ranks[k] = jnp.where(hit, float(it), ranks[k])
            curs[k] = jnp.where(hit, -jnp.inf, curs[k])
    return [(jnp.concatenate(v, axis=0), r) for v, r in zip(vals, ranks)]


def _pair_sum_candidates(v1, v2):
    k = v1.shape[0]
    pieces = []
    for a in range(k // 2):
        nb = min(k, -(-(k // (a + 1)) // 8) * 8)
        pieces.append(v1[a:a + 1] + v2[:nb])
    pieces.append(v1[k // 2:] + v2[0:1])
    return jnp.concatenate(pieces, axis=0)


def _peer_routing(pq_ref, sk_ref, cut_ref, e1_ref, rank_ref, e2_ref, n_keys):
    def head_body(h, _):
        lo = pl.multiple_of(h * LANES, LANES)
        st = _dot_nt(sk_ref[h], pq_ref[:, pl.ds(lo, LANES)])
        n_blocks = st.shape[1] // LANES
        s1 = [st[:n_keys, c * LANES:(c + 1) * LANES] for c in range(n_blocks)]
        s2 = [st[n_keys:, c * LANES:(c + 1) * LANES] for c in range(n_blocks)]
        tops = _top_values(s1 + s2, PEER_TOPK)
        v1 = [t[0] for t in tops[:n_blocks]]
        v2 = [t[0] for t in tops[n_blocks:]]
        rank = [t[1] for t in tops[n_blocks:]]
        cand = [_pair_sum_candidates(a, b) for a, b in zip(v1, v2)]
        thr = [t[0][PEER_TOPK - 1:PEER_TOPK] for t in _top_values(cand, PEER_TOPK)]
        for c in range(n_blocks):
            top = v1[c][0:1] + v2[c][0:1]
            z = jnp.sum(jnp.where(cand[c] >= thr[c], jnp.exp(cand[c] - top), 0.0), axis=0, keepdims=True)
            cut = jnp.zeros_like(s1[c])
            for b in range(PEER_TOPK):
                cut = cut + jnp.where(s1[c] + v2[c][b:b + 1] >= thr[c], 1.0, 0.0)
            cut_ref[h, c] = cut
            e1_ref[h, c] = jnp.exp(s1[c] - v1[c][0:1]) * INV_SQRT2
            rank_ref[h, c] = rank[c]
            e2_ref[h, c] = jnp.exp(s2[c] - v2[c][0:1]) / z
        return 0

    lax.fori_loop(0, PEER_HEADS, head_body, 0, unroll=2)


def _gelu_scaled(y):
    return y * (1.0 + lax.erf(y))


def _peer_kernel(h1_ref, hn_ref, pq_ref, sk_ref, u_ref, vt_ref, o_ref,
                 cut_ref, e1_ref, rank_ref, e2_ref, act_ref, p_ref, acc_ref, *, te, n_keys):
    s = pl.program_id(1)
    tm = hn_ref.shape[0]
    rows_per_tile = te // n_keys

    @pl.when(s == 0)
    def _():
        _peer_routing(pq_ref, sk_ref, cut_ref, e1_ref, rank_ref, e2_ref, n_keys)
        acc_ref[...] = jnp.zeros_like(acc_ref)

    act_ref[...] = _dot_nt(u_ref[...], hn_ref[...])

    def chunk_body(c, _):
        cols = pl.ds(pl.multiple_of(c * LANES, LANES), LANES)
        for row8 in range(rows_per_tile // 8):
            rows = pl.ds(pl.multiple_of(s * rows_per_tile + row8 * 8, 8), 8)
            cut8 = [cut_ref[h, c, rows, :] for h in range(PEER_HEADS)]
            e18 = [e1_ref[h, c, rows, :] for h in range(PEER_HEADS)]
            for k in range(8):
                row = row8 * 8 + k
                w = jnp.zeros((n_keys, LANES), F32)
                for h in range(PEER_HEADS):
                    cut_b = jnp.broadcast_to(cut8[h][k:k + 1], (n_keys, LANES))
                    e1_b = jnp.broadcast_to(e18[h][k:k + 1], (n_keys, LANES))
                    w = w + jnp.where(rank_ref[h, c] < cut_b, e2_ref[h, c], 0.0) * e1_b
                a = act_ref[row * n_keys:(row + 1) * n_keys, cols]
                p_ref[row * n_keys:(row + 1) * n_keys, cols] = (w * _gelu_scaled(a)).astype(BF16)
        return 0

    lax.fori_loop(0, tm // LANES, chunk_body, 0)
    acc_ref[...] += _dot(vt_ref[...], p_ref[...])

    @pl.when(s == pl.num_programs(1) - 1)
    def _():
        o_ref[...] = h1_ref[...] + acc_ref[...].T


def _peer(h1, hn, pq, sub_keys, peer_u, peer_v):
    n_tok, d_model = h1.shape
    n_heads, _, n_keys, half = sub_keys.shape
    n_experts = peer_u.shape[0]
    assert n_heads == PEER_HEADS and n_keys == LANES and 2 * half == LANES and n_experts == n_keys * n_keys
    tm = _pick_tile(n_tok, (512, 384, 256, 128))
    te = 2048
    zeros = jnp.zeros((n_heads, n_keys, half), sub_keys.dtype)
    sk = jnp.concatenate([jnp.concatenate([sub_keys[:, 0], zeros], axis=2),
                          jnp.concatenate([zeros, sub_keys[:, 1]], axis=2)], axis=1).astype(BF16)
    u = peer_u.astype(BF16)
    vt = peer_v.astype(BF16).T
    row = lambda i, j: (i, 0)
    route = pltpu.VMEM((PEER_HEADS, tm // LANES, n_keys, LANES), F32)
    return pl.pallas_call(
        functools.partial(_peer_kernel, te=te, n_keys=n_keys),
        grid=(n_tok // tm, n_experts // te),
        in_specs=[pl.BlockSpec((tm, d_model), row), pl.BlockSpec((tm, d_model), row),
                  pl.BlockSpec((tm, PEER_HEADS * LANES), row),
                  pl.BlockSpec((n_heads, 2 * n_keys, LANES), lambda i, s: (0, 0, 0)),
                  pl.BlockSpec((te, d_model), lambda i, s: (s, 0)),
                  pl.BlockSpec((d_model, te), lambda i, s: (0, s))],
        out_specs=pl.BlockSpec((tm, d_model), row),
        out_shape=jax.ShapeDtypeStruct((n_tok, d_model), F32),
        scratch_shapes=[route, route, route, route, pltpu.VMEM((te, tm), F32),
                        pltpu.VMEM((te, tm), BF16), pltpu.VMEM((d_model, tm), F32)],
        compiler_params=_params(("parallel", "arbitrary")),
        name="peer",
    )(h1, hn, pq, sk, u, vt)


def kernel(x, meta_tokens, mix_norm_gain, w_in, a_q_norm_gain, a_k_norm_gain, a_lambda_q1, a_lambda_k1, a_lambda_q2, a_lambda_k2, a_subln_gain, w_branch_a, b_q_norm_gain, b_k_norm_gain, w_branch_b, w_out, ffn_norm_gain, peer_w_query, peer_sub_keys, peer_u, peer_v):
    bsz, seq, d_model = x.shape
    depth = w_in.shape[0]
    t_real = N_META_TOKENS + seq
    t_pad = ((t_real + SEQ_BLOCK - 1) // SEQ_BLOCK) * SEQ_BLOCK
    k_sel = min(TOPK_LIMIT, seq // 4)

    meta = jnp.broadcast_to(meta_tokens.astype(x.dtype)[None], (bsz, N_META_TOKENS, d_model))
    pad = jnp.zeros((bsz, t_pad - t_real, d_model), x.dtype)
    h = jnp.concatenate([meta, x, pad], axis=1).reshape(bsz * t_pad, d_model)

    q_scale = HEAD_DIM ** -0.5 * LOG2_E
    for layer in range(depth):
        lambda_init = 0.8 - 0.6 * math.exp(-0.3 * layer)
        aqg = (jnp.tile(a_q_norm_gain[layer], 8) * q_scale).reshape(1, 512)
        akg = jnp.tile(a_k_norm_gain[layer], 8).reshape(1, 512)
        bqg = (jnp.tile(b_q_norm_gain[layer], 8) * q_scale).reshape(1, 512)
        bkg = jnp.tile(b_k_norm_gain[layer], 2).reshape(1, LANES)
        p = _input_projection(h, t_pad, mix_norm_gain[layer], w_in[layer], aqg, akg, bqg, bkg)
        vec = lambda a: a[layer].astype(F32).reshape(1, HEAD_DIM)
        a_out = _diff_attention(p["aq"], p["ak"], p["av"], bsz, t_pad, vec(a_lambda_q1), vec(a_lambda_k1),
                                vec(a_lambda_q2), vec(a_lambda_k2), a_subln_gain[layer].reshape(1, LANES), lambda_init)
        b_out = _dsa_attention(p["iq"], p["iw"], p["ik"], p["bq"], p["bk"], p["bv"], bsz, t_pad, k_sel)
        h1, hn, pq = _merge(h, a_out, b_out, p["ga"], p["gb"], w_branch_a[layer], w_branch_b[layer], w_out[layer],
                            ffn_norm_gain[layer], peer_w_query[layer])
        h = _peer(h1, hn, pq, peer_sub_keys[layer], peer_u[layer], peer_v[layer])

    return h.reshape(bsz, t_pad, d_model)[:, N_META_TOKENS:N_META_TOKENS + seq]
```

```python
import functools
import math

import jax
import jax.numpy as jnp
from jax import lax
from jax.experimental import pallas as pl
from jax.experimental.pallas import tpu as pltpu

F32 = jnp.float32
BF16 = jnp.bfloat16
I32 = jnp.int32

N_META_TOKENS = 16
SEQ_BLOCK = 128
ROPE_BASE = 10000.0
NORM_EPS = 1e-6
HEAD_DIM = 64
LANES = 128
A_HEADS = 4
B_HEADS = 8
IDX_HEADS = 8
IDX_ROPE = 32
TOPK_LIMIT = 256
PEER_HEADS = 8
PEER_TOPK = 16
MASKED_SCORE = -1e30
RUNNING_MAX_FLOOR = -1e29
INT_MIN = -(2 ** 31)
LOG2_E = math.log2(math.e)
INV_SQRT2 = 1.0 / math.sqrt(2.0)
VMEM_LIMIT = 56 * 1024 * 1024


def _pick_tile(n, candidates):
    for c in candidates:
        if n % c == 0:
            return c
    raise ValueError(f"no tile in {candidates} divides {n}")


def _params(semantics):
    return pltpu.CompilerParams(dimension_semantics=semantics, vmem_limit_bytes=VMEM_LIMIT)


def _dot(a, b):
    return jnp.dot(a, b, preferred_element_type=F32)


def _dot_nt(a, b):
    return lax.dot_general(a, b, (((1,), (1,)), ((), ())), preferred_element_type=F32)


def _lane_iota(shape):
    return lax.broadcasted_iota(I32, shape, 1)


_SEC = {}
_off = 0
for _name, _w in (("aq", 512), ("ak", 512), ("av", 512), ("bq", 512), ("bk", 128), ("bv", 128),
                  ("iq", 512), ("ik", 128), ("iw", 128), ("ga", 1024), ("gb", 1024)):
    _SEC[_name] = (_off, _off + _w)
    _off += _w
IN_COLS = _off


def _head_norm(x, gain, ones_blk):
    w = x.shape[1]
    x2 = x * x
    hi = x2.astype(BF16)
    lo = (x2 - hi.astype(F32)).astype(BF16)
    m = ones_blk[:w, :w]
    ms = _dot(hi, m) + _dot(lo, m)
    return x * lax.rsqrt(ms + NORM_EPS) * gain


def _rotary(x, cos, sin_signed, half):
    w = x.shape[1]
    lane = _lane_iota(x.shape)
    fwd = pltpu.roll(x, w - half, 1)
    bwd = pltpu.roll(x, half, 1)
    partner = jnp.where((lane & half) == 0, fwd, bwd)
    return x * cos + partner * sin_signed


def _inproj_kernel(h_ref, gain_ref, w_ref, ones_ref, cos_ref, sin_ref, icos_ref, isin_ref,
                   aqg_ref, akg_ref, bqg_ref, bkg_ref,
                   aq_o, ak_o, av_o, bq_o, bk_o, bv_o, iq_o, ik_o, iw_o, ga_o, gb_o, *, iw_scale):
    h = h_ref[...]
    xn = (h * lax.rsqrt(jnp.mean(h * h, axis=-1, keepdims=True) + NORM_EPS) * gain_ref[...]).astype(BF16)

    def proj(name):
        a, b = _SEC[name]
        return _dot(xn, w_ref[:, a:b])

    ones_blk = ones_ref[...]
    cos1 = cos_ref[...]
    sin1 = sin_ref[...]
    cos = jnp.concatenate([cos1] * 4, axis=1)
    sin = jnp.concatenate([sin1] * 4, axis=1)
    aq_o[...] = _rotary(_head_norm(proj("aq"), aqg_ref[...], ones_blk), cos, sin, 32).astype(BF16)
    ak_o[...] = _rotary(_head_norm(proj("ak"), akg_ref[...], ones_blk), cos, sin, 32).astype(BF16)
    av_o[...] = proj("av").astype(BF16)
    bq_o[...] = _rotary(_head_norm(proj("bq"), bqg_ref[...], ones_blk), cos, sin, 32).astype(BF16)
    bk_o[...] = _rotary(_head_norm(proj("bk"), bkg_ref[...], ones_blk), cos1, sin1, 32).astype(BF16)
    bv = proj("bv")
    bv_o[...] = jnp.where(_lane_iota(bv.shape) < HEAD_DIM, bv, 1.0).astype(BF16)
    icos1 = icos_ref[...]
    isin1 = isin_ref[...]
    iq_o[...] = _rotary(proj("iq"), jnp.concatenate([icos1] * 4, axis=1), jnp.concatenate([isin1] * 4, axis=1),
                        16).astype(BF16)
    ik_o[...] = _rotary(proj("ik"), icos1, isin1, 16).astype(BF16)
    iw_o[...] = proj("iw") * iw_scale
    ga_o[...] = jax.nn.sigmoid(proj("ga")).astype(BF16)
    gb_o[...] = jax.nn.sigmoid(proj("gb")).astype(BF16)


def _rope_tables(t_len, dim, width_dim):
    inv = ROPE_BASE ** (-jnp.arange(0, dim, 2, dtype=F32) / dim)
    ang = jnp.arange(t_len, dtype=F32)[:, None] * inv[None, :]
    cos, sin = jnp.cos(ang), jnp.sin(ang)
    rest = width_dim - dim
    cos_h = jnp.concatenate([cos, cos, jnp.ones((t_len, rest), F32)], axis=1)
    sin_h = jnp.concatenate([-sin, sin, jnp.zeros((t_len, rest), F32)], axis=1)
    return cos_h, sin_h


def _input_projection(h2d, t_pad, mix_gain, w_in, aqg, akg, bqg, bkg):
    n_tok, d_model = h2d.shape
    tm = _pick_tile(t_pad, (384, 256, 128))
    per_seq = t_pad // tm

    sizes = (512, 512, 512, 512, 64, 64, 512, 64, IDX_HEADS, d_model, d_model)
    offs = [0]
    for s in sizes:
        offs.append(offs[-1] + s)
    cols = [w_in[:, offs[i]:offs[i + 1]] for i in range(len(sizes))]
    waq, wak, wav, wbq, wbk, wbv, wiq, wik, wiw, wga, wgb = cols
    wiw = jnp.pad(wiw, ((0, 0), (0, LANES - IDX_HEADS)))
    w_all = jnp.concatenate([waq, wak, wav, wbq, wbk, wbk, wbv, wbv, wiq, wik, wik, wiw, wga, wgb],
                            axis=1).astype(BF16)
    assert w_all.shape[1] == IN_COLS

    cos_h, sin_h = _rope_tables(t_pad, HEAD_DIM, HEAD_DIM)
    cos = jnp.tile(cos_h, (1, 2))
    sin = jnp.tile(sin_h, (1, 2))
    icos_h, isin_h = _rope_tables(t_pad, IDX_ROPE, HEAD_DIM)
    icos = jnp.tile(icos_h, (1, 2))
    isin = jnp.tile(isin_h, (1, 2))
    head_id = jnp.arange(512) // HEAD_DIM
    ones_blk = jnp.where(head_id[:, None] == head_id[None, :], 1.0 / HEAD_DIM, 0.0).astype(BF16)

    row = lambda i: (i, 0)
    fixed = lambda i: (0, 0)
    tab = lambda i: (i % per_seq, 0)
    widths = dict(aq=512, ak=512, av=512, bq=512, bk=128, bv=128, iq=512, ik=128, iw=128, ga=d_model, gb=d_model)
    out_dtypes = dict(iw=F32)
    names = ("aq", "ak", "av", "bq", "bk", "bv", "iq", "ik", "iw", "ga", "gb")
    out_shape = tuple(jax.ShapeDtypeStruct((n_tok, widths[k]), out_dtypes.get(k, BF16)) for k in names)
    out_specs = tuple(pl.BlockSpec((tm, widths[k]), row) for k in names)
    in_specs = [
        pl.BlockSpec((tm, d_model), row),
        pl.BlockSpec((1, d_model), fixed),
        pl.BlockSpec((d_model, IN_COLS), fixed),
        pl.BlockSpec((512, 512), fixed),
        pl.BlockSpec((tm, LANES), tab), pl.BlockSpec((tm, LANES), tab),
        pl.BlockSpec((tm, LANES), tab), pl.BlockSpec((tm, LANES), tab),
        pl.BlockSpec((1, 512), fixed), pl.BlockSpec((1, 512), fixed),
        pl.BlockSpec((1, 512), fixed), pl.BlockSpec((1, LANES), fixed),
    ]
    iw_scale = IDX_HEADS ** -0.5 * HEAD_DIM ** -0.5
    outs = pl.pallas_call(
        functools.partial(_inproj_kernel, iw_scale=iw_scale),
        grid=(n_tok // tm,),
        in_specs=in_specs, out_specs=out_specs, out_shape=out_shape,
        compiler_params=_params(("parallel",)),
        name="inproj",
    )(h2d, mix_gain.reshape(1, d_model), w_all, ones_blk, cos, sin, icos, isin, aqg, akg, bqg, bkg)
    return dict(zip(names, outs))


def _stack_heads(x):
    rows, width = x.shape
    lane = _lane_iota((rows, LANES))
    zero = jnp.zeros((rows, LANES), x.dtype)
    groups = []
    for c in range(width // LANES):
        blk = x[:, c * LANES:(c + 1) * LANES]
        groups.append(jnp.where(lane < HEAD_DIM, blk, zero))
        groups.append(jnp.where(lane >= HEAD_DIM, blk, zero))
    return jnp.concatenate(groups, axis=0)


def _diff_kernel(q_ref, k_ref, v_ref, lq1_ref, lk1_ref, lq2_ref, lk2_ref, gain_ref, o_ref,
                 qs_ref, s_ref, p_ref, m_ref, l_ref, alpha_ref, acc_ref, *, tq, tk, lambda_init):
    assert tq == tk
    i = pl.program_id(1)
    rows = 2 * tq
    lane_blocks = tk // LANES
    heads = [(h, slice(h * LANES, (h + 1) * LANES)) for h in range(A_HEADS)]
    for h, cols in heads:
        qs_ref[h] = _stack_heads(q_ref[:, cols])
    m_ref[...] = jnp.full(m_ref.shape, RUNNING_MAX_FLOOR, F32)
    l_ref[...] = jnp.zeros(l_ref.shape, F32)
    acc_ref[...] = jnp.zeros(acc_ref.shape, F32)
    blocks = [slice(r0, r0 + SEQ_BLOCK) for r0 in range(0, rows, SEQ_BLOCK)]

    def chunk(j, on_diagonal):
        start = pl.multiple_of(j * tk, tk)
        for h, cols in heads:
            s_ref[h] = _dot_nt(qs_ref[h], k_ref[pl.ds(start, tk), cols])
        for h, _ in heads:
            for blk in blocks:
                s = s_ref[h, blk, :]
                if on_diagonal:
                    qpos = blk.start % tq + lax.broadcasted_iota(I32, (SEQ_BLOCK, 1), 0)
                    s = jnp.where(_lane_iota((SEQ_BLOCK, tk)) <= qpos, s, MASKED_SCORE)
                    s_ref[h, blk, :] = s
                m_prev = m_ref[h, blk, :]
                m_new = jnp.maximum(m_prev, jnp.max(s, axis=1, keepdims=True))
                alpha_ref[h, blk, :] = jnp.exp2(m_prev - m_new)
                m_ref[h, blk, :] = m_new
        for h, cols in heads:
            for blk in blocks:
                p = jnp.exp2(s_ref[h, blk, :] - jnp.concatenate([m_ref[h, blk, :]] * lane_blocks, axis=1))
                l_ref[h, blk, :] = alpha_ref[h, blk, :] * l_ref[h, blk, :] + jnp.sum(p, axis=1, keepdims=True)
                p_ref[h, blk, :] = p.astype(BF16)
            acc_ref[h] = alpha_ref[h] * acc_ref[h] + _dot(p_ref[h], v_ref[pl.ds(start, tk), cols])

    def body(j, _):
        chunk(j, on_diagonal=False)
        return 0

    lax.fori_loop(0, i, body, 0)
    chunk(i, on_diagonal=True)
    lam = (jnp.exp(jnp.sum(lq1_ref[...] * lk1_ref[...], axis=-1, keepdims=True))
           - jnp.exp(jnp.sum(lq2_ref[...] * lk2_ref[...], axis=-1, keepdims=True)) + lambda_init)
    for h, cols in heads:
        o = acc_ref[h] / l_ref[h]
        d = o[:tq] - lam * o[tq:]
        d = d * lax.rsqrt(jnp.mean(d * d, axis=-1, keepdims=True) + NORM_EPS) * gain_ref[...]
        o_ref[:, cols] = (d * (1.0 - lambda_init)).astype(BF16)


def _diff_attention(aq, ak, av, bsz, t_pad, lq1, lk1, lq2, lk2, subln_gain, lambda_init):
    tq = _pick_tile(t_pad, (384, 256, 128))
    tk = tq
    nq = t_pad // tq
    width = A_HEADS * LANES
    qmap = lambda b, i: (b * nq + i, 0)
    kvmap = lambda b, i: (b, 0)
    fixed = lambda b, i: (0, 0)
    vec = pl.BlockSpec((1, HEAD_DIM), fixed)
    stats = pltpu.VMEM((A_HEADS, 2 * tq, LANES), F32)
    return pl.pallas_call(
        functools.partial(_diff_kernel, tq=tq, tk=tk, lambda_init=lambda_init),
        grid=(bsz, nq),
        in_specs=[pl.BlockSpec((tq, width), qmap), pl.BlockSpec((t_pad, width), kvmap),
                  pl.BlockSpec((t_pad, width), kvmap), vec, vec, vec, vec, pl.BlockSpec((1, LANES), fixed)],
        out_specs=pl.BlockSpec((tq, width), qmap),
        out_shape=jax.ShapeDtypeStruct(aq.shape, BF16),
        scratch_shapes=[pltpu.VMEM((A_HEADS, 2 * tq, LANES), BF16), pltpu.VMEM((A_HEADS, 2 * tq, tk), F32),
                        pltpu.VMEM((A_HEADS, 2 * tq, tk), BF16), stats, stats, stats, stats],
        compiler_params=_params(("parallel", "arbitrary")),
        name="diff_attn",
    )(aq, ak, av, lq1, lk1, lq2, lk2, subln_gain)


def _dsa_kernel(iq_ref, iw_ref, ik_ref, bq_ref, bk_ref, bv_ref, tri_ref, o_ref,
                key_ref, iqs_ref, qs_ref, st_ref, s_ref, p_ref, bias_ref, m_ref, alpha_ref, acc_ref, *, tq, tk, k_sel):
    i = pl.program_id(1)
    n_chunks = ((i + 1) * tq + tk - 1) // tk
    lane_blocks = tk // LANES
    qpos_row = i * tq + _lane_iota((1, tq))

    iqs_ref[...] = _stack_heads(iq_ref[...])
    iw_t = iw_ref[...].T

    def score_body(j, _):
        start = pl.multiple_of(j * tk, tk)
        st_ref[...] = _dot_nt(ik_ref[pl.ds(start, tk), :], iqs_ref[...])
        for r0 in range(0, tk, SEQ_BLOCK):
            score = jnp.zeros((SEQ_BLOCK, tq), F32)
            for h in range(IDX_HEADS):
                logits = st_ref[r0:r0 + SEQ_BLOCK, h * tq:(h + 1) * tq]
                score = score + jnp.maximum(logits, 0.0) * iw_t[h:h + 1, :]
            kpos = j * tk + r0 + lax.broadcasted_iota(I32, (SEQ_BLOCK, 1), 0)
            score = jnp.where(kpos <= qpos_row, score, -jnp.inf)
            bits = lax.bitcast_convert_type(score, I32)
            key_ref[pl.ds(start + r0, SEQ_BLOCK), :] = jnp.where(bits < 0, (bits ^ 0x7FFFFFFF) + 1, bits)
        return 0

    lax.fori_loop(0, n_chunks, score_body, 0)

    def count_ge(cand_row):
        cand = jnp.broadcast_to(cand_row, (8, tq))

        def body(j, acc):
            start = pl.multiple_of(j * tk, tk)
            for r0 in range(0, tk, 8):
                acc = acc + jnp.where(key_ref[pl.ds(start + r0, 8), :] >= cand, 1.0, 0.0)
            return acc

        acc = lax.fori_loop(0, n_chunks, body, jnp.zeros((8, tq), F32))
        return jnp.sum(acc, axis=0, keepdims=True)

    kf = float(k_sel)
    n_nonneg = count_ge(jnp.zeros((1, tq), I32))
    thr = jnp.where(n_nonneg >= kf, 0, INT_MIN).astype(I32)
    n_above = jnp.where(n_nonneg >= kf, 0.0, n_nonneg)

    def bit_body(t, carry):
        thr, n_above = carry
        cand = thr + jnp.left_shift(jnp.int32(1), 30 - t)
        n = count_ge(cand)
        return jnp.where(n >= kf, cand, thr), jnp.where(n >= kf, n_above, n)

    thr, n_above = lax.fori_loop(0, 31, bit_body, (thr, n_above))
    ties_kept = kf - n_above

    qs_ref[...] = _stack_heads(bq_ref[...])
    m_ref[...] = jnp.full(m_ref.shape, RUNNING_MAX_FLOOR, F32)
    acc_ref[...] = jnp.zeros(acc_ref.shape, F32)

    def attn_body(j, ties_before):
        start = pl.multiple_of(j * tk, tk)
        key = key_ref[pl.ds(start, tk), :]
        tie = key == thr
        tie_count = ties_before + _dot(tri_ref[...], jnp.where(tie, 1.0, 0.0).astype(BF16))
        kpos = j * tk + lax.broadcasted_iota(I32, (tk, 1), 0)
        sel = ((key > thr) | (tie & (tie_count <= ties_kept))) & (kpos <= qpos_row)
        bias_ref[...] = jnp.where(sel, 0.0, MASKED_SCORE).T
        s_ref[...] = _dot_nt(qs_ref[...], bk_ref[pl.ds(start, tk), :])
        blocks = [(slice(h * tq + r0, h * tq + r0 + SEQ_BLOCK), slice(r0, r0 + SEQ_BLOCK))
                  for h in range(B_HEADS) for r0 in range(0, tq, SEQ_BLOCK)]
        for rows, qrows in blocks:
            s = s_ref[rows, :] + bias_ref[qrows, :]
            m_prev = m_ref[rows, :]
            m_new = jnp.maximum(m_prev, jnp.max(s, axis=1, keepdims=True))
            alpha_ref[rows, :] = jnp.exp2(m_prev - m_new)
            m_ref[rows, :] = m_new
        for rows, qrows in blocks:
            m_new = jnp.concatenate([m_ref[rows, :]] * lane_blocks, axis=1)
            p_ref[rows, :] = jnp.exp2(s_ref[rows, :] + bias_ref[qrows, :] - m_new).astype(BF16)
        acc_ref[...] = alpha_ref[...] * acc_ref[...] + _dot(p_ref[...], bv_ref[pl.ds(start, tk), :])
        return tie_count[tk - 1:tk, :]

    lax.fori_loop(0, n_chunks, attn_body, jnp.zeros((1, tq), F32))
    lane = _lane_iota((tq, LANES))
    for c in range(B_HEADS // 2):
        even = acc_ref[(2 * c) * tq:(2 * c + 1) * tq, :]
        odd = acc_ref[(2 * c + 1) * tq:(2 * c + 2) * tq, :]
        o_ref[:, c * LANES:(c + 1) * LANES] = jnp.where(
            lane < HEAD_DIM, even / pltpu.roll(even, HEAD_DIM, 1), pltpu.roll(odd, HEAD_DIM, 1) / odd).astype(BF16)


def _dsa_attention(iq, iw, ik, bq, bk, bv, bsz, t_pad, k_sel):
    tq = _pick_tile(t_pad, (384, 256, 128))
    tk = tq
    nq = t_pad // tq
    qmap = lambda b, i: (b * nq + i, 0)
    kvmap = lambda b, i: (b, 0)
    tri = (jnp.arange(tk)[:, None] >= jnp.arange(tk)[None, :]).astype(BF16)
    stacked = pltpu.VMEM((B_HEADS * tq, LANES), BF16)
    stats = pltpu.VMEM((B_HEADS * tq, LANES), F32)
    return pl.pallas_call(
        functools.partial(_dsa_kernel, tq=tq, tk=tk, k_sel=k_sel),
        grid=(bsz, nq),
        in_specs=[pl.BlockSpec((tq, 512), qmap), pl.BlockSpec((tq, LANES), qmap), pl.BlockSpec((t_pad, LANES), kvmap),
                  pl.BlockSpec((tq, 512), qmap), pl.BlockSpec((t_pad, LANES), kvmap), pl.BlockSpec((t_pad, LANES), kvmap),
                  pl.BlockSpec((tk, tk), lambda b, i: (0, 0))],
        out_specs=pl.BlockSpec((tq, 512), qmap),
        out_shape=jax.ShapeDtypeStruct(bq.shape, BF16),
        scratch_shapes=[pltpu.VMEM((t_pad, tq), I32), stacked, stacked,
                        pltpu.VMEM((tk, B_HEADS * tq), F32), pltpu.VMEM((B_HEADS * tq, tk), F32), pltpu.VMEM((B_HEADS * tq, tk), BF16),
                        pltpu.VMEM((tq, tk), F32), stats, stats, stats],
        compiler_params=_params(("parallel", "arbitrary")),
        name="dsa_attn",
    )(iq, iw, ik, bq, bk, bv, tri)


def _merge_kernel(h_ref, a_ref, b_ref, ga_ref, gb_ref, wa_ref, wb_ref, wo_ref, gain_ref, wq_ref,
                  h1_o, hn_o, pq_o):
    ya = _dot(a_ref[...], wa_ref[...])
    yb = _dot(b_ref[...], wb_ref[...])
    mixed = ga_ref[...].astype(F32) * ya + gb_ref[...].astype(F32) * yb
    h1 = h_ref[...] + _dot(mixed.astype(BF16), wo_ref[...])
    h1_o[...] = h1
    hn = h1 * lax.rsqrt(jnp.mean(h1 * h1, axis=-1, keepdims=True) + NORM_EPS) * gain_ref[...]
    hn_o[...] = (hn * INV_SQRT2).astype(BF16)
    pq_o[...] = _dot(hn.astype(BF16), wq_ref[...]).astype(BF16)


def _merge(h2d, a_out, b_out, ga, gb, w_a, w_b, w_o, ffn_gain, w_query):
    n_tok, d_model = h2d.shape
    tm = _pick_tile(n_tok, (256, 128))
    row = lambda i: (i, 0)
    fixed = lambda i: (0, 0)
    pw = w_query.shape[1]
    return pl.pallas_call(
        _merge_kernel,
        grid=(n_tok // tm,),
        in_specs=[pl.BlockSpec((tm, d_model), row), pl.BlockSpec((tm, 512), row), pl.BlockSpec((tm, 512), row),
                  pl.BlockSpec((tm, d_model), row), pl.BlockSpec((tm, d_model), row),
                  pl.BlockSpec((512, d_model), fixed), pl.BlockSpec((512, d_model), fixed),
                  pl.BlockSpec((d_model, d_model), fixed), pl.BlockSpec((1, d_model), fixed),
                  pl.BlockSpec((d_model, pw), fixed)],
        out_specs=(pl.BlockSpec((tm, d_model), row), pl.BlockSpec((tm, d_model), row), pl.BlockSpec((tm, pw), row)),
        out_shape=(jax.ShapeDtypeStruct((n_tok, d_model), F32), jax.ShapeDtypeStruct((n_tok, d_model), BF16),
                   jax.ShapeDtypeStruct((n_tok, pw), BF16)),
        compiler_params=_params(("parallel",)),
        name="merge",
    )(h2d, a_out, b_out, ga, gb, w_a.astype(BF16), w_b.astype(BF16), w_o.astype(BF16),
      ffn_gain.reshape(1, d_model), w_query.astype(BF16))


def _top_values(xs, count):
    curs = list(xs)
    ranks = [jnp.full(x.shape, float(count), F32) for x in xs]
    vals = [[] for _ in xs]
    for it in range(count):
        for k in range(len(curs)):
            m = jnp.max(curs[k], axis=0, keepdims=True)
            vals[k].append(m)
            hit = curs[k] == m
            ranks[k] = jnp.where(hit, float(it), ranks[k])
            curs[k] = jnp.where(hit, -jnp.inf, curs[k])
    return [(jnp.concatenate(v, axis=0), r) for v, r in zip(vals, ranks)]


def _pair_sum_candidates(v1, v2):
    k = v1.shape[0]
    pieces = []
    for a in range(k // 2):
        nb = min(k, -(-(k // (a + 1)) // 8) * 8)
        pieces.append(v1[a:a + 1] + v2[:nb])
    pieces.append(v1[k // 2:] + v2[0:1])
    return jnp.concatenate(pieces, axis=0)


def _peer_routing(pq_ref, sk_ref, cut_ref, e1_ref, rank_ref, e2_ref, n_keys):
    def head_body(h, _):
        lo = pl.multiple_of(h * LANES, LANES)
        st = _dot_nt(sk_ref[h], pq_ref[:, pl.ds(lo, LANES)])
        n_blocks = st.shape[1] // LANES
        s1 = [st[:n_keys, c * LANES:(c + 1) * LANES] for c in range(n_blocks)]
        s2 = [st[n_keys:, c * LANES:(c + 1) * LANES] for c in range(n_blocks)]
        tops = _top_values(s1 + s2, PEER_TOPK)
        v1 = [t[0] for t in tops[:n_blocks]]
        v2 = [t[0] for t in tops[n_blocks:]]
        rank = [t[1] for t in tops[n_blocks:]]
        cand = [_pair_sum_candidates(a, b) for a, b in zip(v1, v2)]
        thr = [t[0][PEER_TOPK - 1:PEER_TOPK] for t in _top_values(cand, PEER_TOPK)]
        for c in range(n_blocks):
            top = v1[c][0:1] + v2[c][0:1]
            z = jnp.sum(jnp.where(cand[c] >= thr[c], jnp.exp(cand[c] - top), 0.0), axis=0, keepdims=True)
            cut = jnp.zeros_like(s1[c])
            for b in range(PEER_TOPK):
                cut = cut + jnp.where(s1[c] + v2[c][b:b + 1] >= thr[c], 1.0, 0.0)
            cut_ref[h, c] = cut
            e1_ref[h, c] = jnp.exp(s1[c] - v1[c][0:1]) * INV_SQRT2
            rank_ref[h, c] = rank[c]
            e2_ref[h, c] = jnp.exp(s2[c] - v2[c][0:1]) / z
        return 0

    lax.fori_loop(0, PEER_HEADS, head_body, 0, unroll=4)


def _gelu_scaled(y):
    return y * (1.0 + lax.erf(y))


def _peer_kernel(h1_ref, hn_ref, pq_ref, sk_ref, u_ref, vt_ref, o_ref,
                 cut_ref, e1_ref, rank_ref, e2_ref, act_ref, p_ref, acc_ref, *, te, n_keys):
    s = pl.program_id(1)
    tm = hn_ref.shape[0]
    rows_per_tile = te // n_keys

    @pl.when(s == 0)
    def _():
        _peer_routing(pq_ref, sk_ref, cut_ref, e1_ref, rank_ref, e2_ref, n_keys)
        acc_ref[...] = jnp.zeros_like(acc_ref)

    act_ref[...] = _dot_nt(u_ref[...], hn_ref[...])

    def chunk_body(c, _):
        cols = pl.ds(pl.multiple_of(c * LANES, LANES), LANES)
        for row8 in range(rows_per_tile // 8):
            rows = pl.ds(pl.multiple_of(s * rows_per_tile + row8 * 8, 8), 8)
            cut8 = [cut_ref[h, c, rows, :] for h in range(PEER_HEADS)]
            e18 = [e1_ref[h, c, rows, :] for h in range(PEER_HEADS)]
            for k in range(8):
                row = row8 * 8 + k
                w = jnp.zeros((n_keys, LANES), F32)
                for h in range(PEER_HEADS):
                    cut_b = jnp.broadcast_to(cut8[h][k:k + 1], (n_keys, LANES))
                    e1_b = jnp.broadcast_to(e18[h][k:k + 1], (n_keys, LANES))
                    w = w + jnp.where(rank_ref[h, c] < cut_b, e2_ref[h, c], 0.0) * e1_b
                a = act_ref[row * n_keys:(row + 1) * n_keys, cols]
                p_ref[row * n_keys:(row + 1) * n_keys, cols] = (w * _gelu_scaled(a)).astype(BF16)
        return 0

    lax.fori_loop(0, tm // LANES, chunk_body, 0)
    acc_ref[...] += _dot(vt_ref[...], p_ref[...])

    @pl.when(s == pl.num_programs(1) - 1)
    def _():
        o_ref[...] = h1_ref[...] + acc_ref[...].T


def _peer(h1, hn, pq, sub_keys, peer_u, peer_v):
    n_tok, d_model = h1.shape
    n_heads, _, n_keys, half = sub_keys.shape
    n_experts = peer_u.shape[0]
    assert n_heads == PEER_HEADS and n_keys == LANES and 2 * half == LANES and n_experts == n_keys * n_keys
    tm = _pick_tile(n_tok, (512, 384, 256, 128))
    te = 2048
    zeros = jnp.zeros((n_heads, n_keys, half), sub_keys.dtype)
    sk = jnp.concatenate([jnp.concatenate([sub_keys[:, 0], zeros], axis=2),
                          jnp.concatenate([zeros, sub_keys[:, 1]], axis=2)], axis=1).astype(BF16)
    u = peer_u.astype(BF16)
    vt = peer_v.astype(BF16).T
    row = lambda i, j: (i, 0)
    route = pltpu.VMEM((PEER_HEADS, tm // LANES, n_keys, LANES), F32)
    return pl.pallas_call(
        functools.partial(_peer_kernel, te=te, n_keys=n_keys),
        grid=(n_tok // tm, n_experts // te),
        in_specs=[pl.BlockSpec((tm, d_model), row), pl.BlockSpec((tm, d_model), row),
                  pl.BlockSpec((tm, PEER_HEADS * LANES), row),
                  pl.BlockSpec((n_heads, 2 * n_keys, LANES), lambda i, s: (0, 0, 0)),
                  pl.BlockSpec((te, d_model), lambda i, s: (s, 0)),
                  pl.BlockSpec((d_model, te), lambda i, s: (0, s))],
        out_specs=pl.BlockSpec((tm, d_model), row),
        out_shape=jax.ShapeDtypeStruct((n_tok, d_model), F32),
        scratch_shapes=[route, route, route, route, pltpu.VMEM((te, tm), F32),
                        pltpu.VMEM((te, tm), BF16), pltpu.VMEM((d_model, tm), F32)],
        compiler_params=_params(("parallel", "arbitrary")),
        name="peer",
    )(h1, hn, pq, sk, u, vt)


def kernel(x, meta_tokens, mix_norm_gain, w_in, a_q_norm_gain, a_k_norm_gain, a_lambda_q1, a_lambda_k1, a_lambda_q2, a_lambda_k2, a_subln_gain, w_branch_a, b_q_norm_gain, b_k_norm_gain, w_branch_b, w_out, ffn_norm_gain, peer_w_query, peer_sub_keys, peer_u, peer_v):
    bsz, seq, d_model = x.shape
    depth = w_in.shape[0]
    t_real = N_META_TOKENS + seq
    t_pad = ((t_real + SEQ_BLOCK - 1) // SEQ_BLOCK) * SEQ_BLOCK
    k_sel = min(TOPK_LIMIT, seq // 4)

    meta = jnp.broadcast_to(meta_tokens.astype(x.dtype)[None], (bsz, N_META_TOKENS, d_model))
    pad = jnp.zeros((bsz, t_pad - t_real, d_model), x.dtype)
    h = jnp.concatenate([meta, x, pad], axis=1).reshape(bsz * t_pad, d_model)

    q_scale = HEAD_DIM ** -0.5 * LOG2_E
    for layer in range(depth):
        lambda_init = 0.8 - 0.6 * math.exp(-0.3 * layer)
        aqg = (jnp.tile(a_q_norm_gain[layer], 8) * q_scale).reshape(1, 512)
        akg = jnp.tile(a_k_norm_gain[layer], 8).reshape(1, 512)
        bqg = (jnp.tile(b_q_norm_gain[layer], 8) * q_scale).reshape(1, 512)
        bkg = jnp.tile(b_k_norm_gain[layer], 2).reshape(1, LANES)
        p = _input_projection(h, t_pad, mix_norm_gain[layer], w_in[layer], aqg, akg, bqg, bkg)
        vec = lambda a: a[layer].astype(F32).reshape(1, HEAD_DIM)
        a_out = _diff_attention(p["aq"], p["ak"], p["av"], bsz, t_pad, vec(a_lambda_q1), vec(a_lambda_k1),
                                vec(a_lambda_q2), vec(a_lambda_k2), a_subln_gain[layer].reshape(1, LANES), lambda_init)
        b_out = _dsa_attention(p["iq"], p["iw"], p["ik"], p["bq"], p["bk"], p["bv"], bsz, t_pad, k_sel)
        h1, hn, pq = _merge(h, a_out, b_out, p["ga"], p["gb"], w_branch_a[layer], w_branch_b[layer], w_out[layer],
                            ffn_norm_gain[layer], peer_w_query[layer])
        h = _peer(h1, hn, pq, peer_sub_keys[layer], peer_u[layer], peer_v[layer])

    return h.reshape(bsz, t_pad, d_model)[:, N_META_TOKENS:N_META_TOKENS + seq]
```
